```python
import jax, jax.numpy as jnp
from jax import lax
import numpy as np

D_MODEL = 1024
BATCH = 4
SEQ = 4096
DEPTH = 2

GRID_W = 64
CTX_LEN = 256
D_LRU = 1024
LRU_HEADS = 16
LRU_HEAD_DIM = D_LRU // LRU_HEADS
CONV_WIDTH = 4
CONV_PAD = (2, 1)
LRU_C = 8.0
D_POOL = 512
POOL_WINDOWS = (2, 4, 8, 16)
POOL_GROUP = D_POOL // len(POOL_WINDOWS)
D_FF = 4 * D_MODEL
N_BRANCH = 2
D_IN = 2 * D_LRU + D_POOL + N_BRANCH * D_MODEL
IN_SPLITS = (D_LRU, 2 * D_LRU, 2 * D_LRU + D_POOL)
N_MOD = 6
EPS = 1e-6

kernel_name = "hybrid_rglru_pool_dit_block"


def rms_norm(x, g):
    xf = x.astype(jnp.float32)
    y = xf * lax.rsqrt(jnp.mean(xf * xf, axis=-1, keepdims=True) + EPS)
    return (y * g.astype(jnp.float32)).astype(x.dtype)


def modulate(x, g, shift, scale):
    return rms_norm(x, g) * (1.0 + scale[:, None, :]) + shift[:, None, :]


def depthwise_conv(u, w, b):
    y = lax.conv_general_dilated(
        u, w[:, None, :].astype(u.dtype), window_strides=(1,), padding=[CONV_PAD],
        dimension_numbers=('NWC', 'WIO', 'NWC'), feature_group_count=u.shape[-1])
    return y + b


def rglru_coeffs(uc, w_r, b_r, w_i, b_i, lam):
    bsz, L, _ = uc.shape
    uh = uc.reshape(bsz, L, LRU_HEADS, LRU_HEAD_DIM)
    r = jax.nn.sigmoid(jnp.einsum('blhd,hde->blhe', uh, w_r).reshape(bsz, L, D_LRU) + b_r)
    i = jax.nn.sigmoid(jnp.einsum('blhd,hde->blhe', uh, w_i).reshape(bsz, L, D_LRU) + b_i)
    log_a = -LRU_C * r.astype(jnp.float32) * jax.nn.softplus(-lam.astype(jnp.float32))
    a = jnp.exp(log_a)
    b = jnp.sqrt(-jnp.expm1(2.0 * log_a)) * (i * uc).astype(jnp.float32)
    return a, b


def linear_scan(a, b, h0, reverse):
    if reverse:
        a, b = jnp.flip(a, axis=1), jnp.flip(b, axis=1)
    b = b.at[:, 0].add(a[:, 0] * h0)

    def combine(e1, e2):
        a1, b1 = e1
        a2, b2 = e2
        return a1 * a2, a2 * b1 + b2

    _, h = lax.associative_scan(combine, (a, b), axis=1)
    if reverse:
        h = jnp.flip(h, axis=1)
    return h


def rglru_bidirectional(u_lat, u_ctx, conv_w, conv_b, w_r, b_r, w_i, b_i, lam):
    uc_l = depthwise_conv(u_lat, conv_w, conv_b)
    uc_c = depthwise_conv(u_ctx, conv_w, conv_b)
    h0 = jnp.zeros((u_ctx.shape[0], D_LRU), jnp.float32)
    out_l = jnp.zeros(uc_l.shape, jnp.float32)
    out_c = jnp.zeros(uc_c.shape, jnp.float32)
    for d, rev in enumerate((False, True)):
        a_c, b_c = rglru_coeffs(uc_c, w_r[d], b_r[d], w_i[d], b_i[d], lam[d])
        h_c = linear_scan(a_c, b_c, h0, rev)
        h_final = h_c[:, 0] if rev else h_c[:, -1]
        a_l, b_l = rglru_coeffs(uc_l, w_r[d], b_r[d], w_i[d], b_i[d], lam[d])
        out_l = out_l + linear_scan(a_l, b_l, h_final, rev)
        out_c = out_c + h_c
    return out_l.astype(u_lat.dtype), out_c.astype(u_ctx.dtype)


def pool_mixer(p, pool_w, pool_scale):
    L = p.shape[-2]
    pf = p.astype(jnp.float32)
    cs = jnp.concatenate([jnp.zeros_like(pf[..., :1, :]), jnp.cumsum(pf, axis=-2)], axis=-2)
    t = jnp.arange(L)
    groups = []
    for gi, w in enumerate(POOL_WINDOWS):
        sl = slice(gi * POOL_GROUP, (gi + 1) * POOL_GROUP)
        lo = jnp.clip(t - w // 2, 0, L)
        hi = jnp.clip(t + w - w // 2, 0, L)
        csg = cs[..., sl]
        mean = (jnp.take(csg, hi, axis=-2) - jnp.take(csg, lo, axis=-2)) / (hi - lo).astype(jnp.float32)[:, None]
        groups.append(mean - pf[..., sl])
    m = jnp.stack(groups, axis=-2)
    y = jnp.einsum('bnlgc,gce->bnlge', m.astype(p.dtype), pool_w)
    return y.reshape(p.shape) * pool_scale


def branch_merge(h_lru, g, p_mixed, gt, w_lru_out, w_pool_out, w_o):
    lru_out = (h_lru * jax.nn.gelu(g)) @ w_lru_out
    pool_out = p_mixed @ w_pool_out
    g_lru, g_pool = jnp.split(jax.nn.sigmoid(gt), N_BRANCH, axis=-1)
    return (g_lru * lru_out + g_pool * pool_out) @ w_o


def sq_relu_mlp(h, w1, w2):
    return jnp.square(jax.nn.relu(h @ w1)) @ w2


def setup_inputs(seed: int = 0) -> dict:
    key = jax.random.key(seed)
    ks = jax.random.split(key, 24)
    f32 = jnp.float32
    D, L_ = D_MODEL, DEPTH

    def nrm(k, shape, scale):
        return jax.random.normal(k, shape, f32) * scale

    u = jax.random.uniform(ks[15], (L_, 2, D_LRU), f32, minval=0.9, maxval=0.999)
    return {
        "x": nrm(ks[0], (BATCH, SEQ, D), 1.0),
        "c": nrm(ks[1], (BATCH, D), 1.0),
        "ctx": nrm(ks[2], (BATCH, CTX_LEN, D), 1.0),
        "c_ctx": nrm(ks[3], (D,), 1.0),
        "w_ada": nrm(ks[4], (L_, D, N_MOD * D), 0.5 * D ** -0.5),
        "b_ada": nrm(ks[5], (L_, N_MOD * D), 0.02),
        "norm1_g": 1.0 + nrm(ks[6], (L_, D), 0.05),
        "norm2_g": 1.0 + nrm(ks[7], (L_, D), 0.05),
        "w_in": nrm(ks[8], (L_, D, D_IN), D ** -0.5),
        "conv_w": nrm(ks[9], (L_, CONV_WIDTH, D_LRU), CONV_WIDTH ** -0.5),
        "conv_b": nrm(ks[10], (L_, D_LRU), 0.02),
        "lru_w_r": nrm(ks[11], (L_, 2, LRU_HEADS, LRU_HEAD_DIM, LRU_HEAD_DIM), LRU_HEAD_DIM ** -0.5),
        "lru_b_r": nrm(ks[12], (L_, 2, D_LRU), 0.02),
        "lru_w_i": nrm(ks[13], (L_, 2, LRU_HEADS, LRU_HEAD_DIM, LRU_HEAD_DIM), LRU_HEAD_DIM ** -0.5),
        "lru_b_i": nrm(ks[14], (L_, 2, D_LRU), 0.02),
        "lru_lambda": jnp.log(u) - jnp.log1p(-u),
        "w_lru_out": nrm(ks[16], (L_, D_LRU, D), D_LRU ** -0.5),
        "pool_w": nrm(ks[17], (L_, len(POOL_WINDOWS), POOL_GROUP, POOL_GROUP), POOL_GROUP ** -0.5),
        "pool_scale": 1.0 + nrm(ks[18], (L_, D_POOL), 0.1),
        "w_pool_out": nrm(ks[19], (L_, D_POOL, D), D_POOL ** -0.5),
        "w_o": nrm(ks[20], (L_, D, D), D ** -0.5),
        "mlp_w1": nrm(ks[21], (L_, D, D_FF), D ** -0.5),
        "mlp_w2": nrm(ks[22], (L_, D_FF, D), D_FF ** -0.5),
        "final_g": 1.0 + nrm(ks[23], (D,), 0.05),
    }


def reference(x, c, ctx, c_ctx, w_ada, b_ada, norm1_g, norm2_g, w_in, conv_w, conv_b,
              lru_w_r, lru_b_r, lru_w_i, lru_b_i, lru_lambda, w_lru_out, pool_w, pool_scale,
              w_pool_out, w_o, mlp_w1, mlp_w2, final_g):
    bsz, seq, _ = x.shape
    rows = seq // GRID_W
    silu_c = jax.nn.silu(c)
    silu_cc = jax.nn.silu(c_ctx)[None]
    for l in range(DEPTH):
        last = l == DEPTH - 1
        shift1, scale1, gate1, shift2, scale2, gate2 = jnp.split(silu_c @ w_ada[l] + b_ada[l], N_MOD, axis=-1)
        cshift1, cscale1, cgate1, cshift2, cscale2, cgate2 = jnp.split(silu_cc @ w_ada[l] + b_ada[l], N_MOD, axis=-1)

        h_l = modulate(x, norm1_g[l], shift1, scale1)
        h_c = modulate(ctx, norm1_g[l], cshift1, cscale1)
        u_l, g_l, p_l, gt_l = jnp.split(h_l @ w_in[l], IN_SPLITS, axis=-1)
        if last:
            u_c = h_c @ w_in[l][:, :D_LRU]
        else:
            u_c, g_c, p_c, gt_c = jnp.split(h_c @ w_in[l], IN_SPLITS, axis=-1)
        hl_lru, hc_lru = rglru_bidirectional(u_l, u_c, conv_w[l], conv_b[l], lru_w_r[l], lru_b_r[l],
                                             lru_w_i[l], lru_b_i[l], lru_lambda[l])
        pm_l = pool_mixer(p_l.reshape(bsz, rows, GRID_W, D_POOL), pool_w[l], pool_scale[l]).reshape(bsz, seq, D_POOL)
        y_l = branch_merge(hl_lru, g_l, pm_l, gt_l, w_lru_out[l], w_pool_out[l], w_o[l])
        x = x + gate1[:, None, :] * y_l
        x = x + gate2[:, None, :] * sq_relu_mlp(modulate(x, norm2_g[l], shift2, scale2), mlp_w1[l], mlp_w2[l])

        if not last:
            pm_c = pool_mixer(p_c[:, None], pool_w[l], pool_scale[l])[:, 0]
            y_c = branch_merge(hc_lru, g_c, pm_c, gt_c, w_lru_out[l], w_pool_out[l], w_o[l])
            ctx = ctx + cgate1[:, None, :] * y_c
            ctx = ctx + cgate2[:, None, :] * sq_relu_mlp(modulate(ctx, norm2_g[l], cshift2, cscale2), mlp_w1[l], mlp_w2[l])
    return rms_norm(x, final_g)
```

```python
import functools

import jax
import jax.numpy as jnp
import numpy as np
from jax import lax
from jax.experimental import pallas as pl
from jax.experimental.pallas import tpu as pltpu

D_MODEL = 1024
D_LRU = 1024
LRU_HEADS = 16
LRU_HEAD_DIM = D_LRU // LRU_HEADS
CONV_WIDTH = 4
CONV_LEFT = 2
LRU_C = 8.0
D_POOL = 512
POOL_WINDOWS = (2, 4, 8, 16)
POOL_GROUP = D_POOL // len(POOL_WINDOWS)
D_FF = 4 * D_MODEL
N_MOD = 6
GRID_W = 64
EPS = 1e-6

SUBLANES = 8
HALO = SUBLANES
GATE_GROUP = 256
N_GATE_GROUPS = D_LRU // GATE_GROUP
SEQ_TILE = 512
MLP_TILE = 1024
FF_CHUNK = 1024
ADA_CHUNK = 1536
VMEM_LIMIT = 56 * 1024 * 1024

F32 = jnp.float32
BF16 = jnp.bfloat16


def _const_spec(shape):
    nd = len(shape)
    return pl.BlockSpec(shape, lambda *_: (0,) * nd, pipeline_mode=pl.Buffered(1))


def _modulate(x, g, scale, shift):
    ms = jnp.mean(x * x, axis=-1, keepdims=True)
    return (x * lax.rsqrt(ms + EPS) * g) * (1.0 + scale) + shift


def _softplus(x):
    return jnp.maximum(x, 0.0) + jnp.log1p(jnp.exp(-jnp.abs(x)))


def _ada_kernel(c_ref, w_ref, b_ref, o_ref):
    c = c_ref[...]
    s = c * jax.nn.sigmoid(c)
    o_ref[0] = jnp.dot(s, w_ref[0], preferred_element_type=F32,
                       precision=lax.Precision.HIGHEST) + b_ref[0]


def _ada_call(cc, w_ada, b_ada):
    depth, d, n = w_ada.shape
    return pl.pallas_call(
        _ada_kernel,
        grid=(depth, n // ADA_CHUNK),
        in_specs=[
            pl.BlockSpec((SUBLANES, d), lambda l, k: (0, 0)),
            pl.BlockSpec((1, d, ADA_CHUNK), lambda l, k: (l, 0, k)),
            pl.BlockSpec((1, 1, ADA_CHUNK), lambda l, k: (l, 0, k)),
        ],
        out_specs=pl.BlockSpec((1, SUBLANES, ADA_CHUNK), lambda l, k: (l, 0, k)),
        out_shape=jax.ShapeDtypeStruct((depth, SUBLANES, n), F32),
        compiler_params=pltpu.CompilerParams(
            dimension_semantics=("arbitrary", "arbitrary"), vmem_limit_bytes=VMEM_LIMIT),
        name="adaln",
    )(cc, w_ada, b_ada.reshape(depth, 1, n))


def _gate_coeffs(uc, wg_ref, br, bi, lam, a_s, b_s):
    sp = _softplus(-lam)
    ucb = uc.astype(BF16)
    for q in range(N_GATE_GROUPS):
        cols = slice(q * GATE_GROUP, (q + 1) * GATE_GROUP)
        z = jnp.dot(ucb[:, cols], wg_ref[q], preferred_element_type=F32)
        r = jax.nn.sigmoid(z[:, :GATE_GROUP] + br[:, cols])
        i = jax.nn.sigmoid(z[:, GATE_GROUP:] + bi[:, cols])
        neg_log_a = LRU_C * r * sp[:, cols]
        a = jnp.exp(-neg_log_a)
        mult = jnp.sqrt(jnp.tanh(neg_log_a) * (1.0 + a * a))
        a_s[:, cols] = a
        b_s[:, cols] = mult * (i * uc[:, cols])


def _scan_tile(a_s, b_s, emit, c0, rows, reverse):
    n_groups = rows // SUBLANES

    def body(g, c):
        gi = (n_groups - 1 - g) if reverse else g
        r0 = pl.multiple_of(gi * SUBLANES, SUBLANES)
        a = a_s[pl.ds(r0, SUBLANES), :]
        b = b_s[pl.ds(r0, SUBLANES), :]
        sub = lax.broadcasted_iota(jnp.int32, a.shape, 0)
        for k in (1, 2, 4):
            shift = (SUBLANES - k) if reverse else k
            m = (sub < SUBLANES - k) if reverse else (sub >= k)
            ra = pltpu.roll(a, shift, 0)
            rb = pltpu.roll(b, shift, 0)
            b = jnp.where(m, a * rb, 0.0) + b
            a = jnp.where(m, a * ra, a)
        h = a * c + b
        emit(r0, h)
        return h[0:1, :] if reverse else h[SUBLANES - 1:SUBLANES, :]

    return lax.fori_loop(0, n_groups, body, c0, unroll=4)


def _front_kernel(x_ref, xp_ref, xn_ref, mod_ref, g1_ref, wu_ref, cw_ref, cb_ref, wg_ref,
                  br_ref, bi_ref, lam_ref, h0_ref, uc_ref, hf_ref, hfin_ref,
                  u_s, a_s, b_s, carry_s, *, tile, n_tiles):
    j = pl.program_id(1)

    @pl.when(j == 0)
    def _():
        carry_s[...] = h0_ref[0]

    mod = mod_ref[0]
    x_ext = jnp.concatenate([xp_ref[0], x_ref[0], xn_ref[0]], axis=0)
    h = _modulate(x_ext, g1_ref[...], mod[1:2], mod[0:1])
    u = jnp.dot(h.astype(BF16), wu_ref[...], preferred_element_type=F32)
    row = lax.broadcasted_iota(jnp.int32, (tile + 2 * HALO, 1), 0)
    valid = ((row >= HALO) | (j > 0)) & ((row < tile + HALO) | (j < n_tiles - 1))
    u_s[...] = jnp.where(valid, u, 0.0)

    cw = cw_ref[...]
    uc = cb_ref[...] + cw[0:1] * u_s[pl.ds(HALO - CONV_LEFT, tile), :]
    for k in range(1, CONV_WIDTH):
        uc = uc + cw[k:k + 1] * u_s[pl.ds(HALO - CONV_LEFT + k, tile), :]
    uc_ref[0] = uc

    _gate_coeffs(uc, wg_ref, br_ref[...], bi_ref[...], lam_ref[...], a_s, b_s)

    def emit(r0, hrows):
        hf_ref[0, pl.ds(r0, SUBLANES), :] = hrows

    c = _scan_tile(a_s, b_s, emit, carry_s[...], tile, reverse=False)
    carry_s[...] = c
    hfin_ref[0] = c


def _front_call(x, mod, mod_row, g1, wu, cw, cb, wg, br, bi, lam, h0, *, tile):
    bsz, seq, d = x.shape
    n_tiles = seq // tile
    hb = tile // HALO
    n_hb = seq // HALO
    tok = lambda b, j: (b, j, 0)
    kern = functools.partial(_front_kernel, tile=tile, n_tiles=n_tiles)
    return pl.pallas_call(
        kern,
        grid=(bsz, n_tiles),
        in_specs=[
            pl.BlockSpec((1, tile, d), tok),
            pl.BlockSpec((1, HALO, d), lambda b, j: (b, jnp.maximum(j * hb - 1, 0), 0)),
            pl.BlockSpec((1, HALO, d), lambda b, j: (b, jnp.minimum((j + 1) * hb, n_hb - 1), 0)),
            pl.BlockSpec((1, N_MOD, d), lambda b, j: (mod_row(b), 0, 0)),
            _const_spec(g1.shape), _const_spec(wu.shape), _const_spec(cw.shape),
            _const_spec(cb.shape), _const_spec(wg.shape), _const_spec(br.shape),
            _const_spec(bi.shape), _const_spec(lam.shape),
            pl.BlockSpec((1, 1, D_LRU), lambda b, j: (b, 0, 0)),
        ],
        out_specs=[
            pl.BlockSpec((1, tile, D_LRU), tok),
            pl.BlockSpec((1, tile, D_LRU), tok),
            pl.BlockSpec((1, 1, D_LRU), lambda b, j: (b, 0, 0)),
        ],
        out_shape=[
            jax.ShapeDtypeStruct((bsz, seq, D_LRU), F32),
            jax.ShapeDtypeStruct((bsz, seq, D_LRU), F32),
            jax.ShapeDtypeStruct((bsz, 1, D_LRU), F32),
        ],
        scratch_shapes=[
            pltpu.VMEM((tile + 2 * HALO, D_LRU), F32),
            pltpu.VMEM((tile, D_LRU), F32),
            pltpu.VMEM((tile, D_LRU), F32),
            pltpu.VMEM((1, D_LRU), F32),
        ],
        compiler_params=pltpu.CompilerParams(
            dimension_semantics=("arbitrary", "arbitrary"), vmem_limit_bytes=VMEM_LIMIT),
        name=f"front_t{tile}",
    )(x, x, x, mod, g1, wu, cw, cb, wg, br, bi, lam, h0)


def _mixer_kernel(*refs, tile, full):
    if full:
        (x_ref, uc_ref, hf_ref, mod_ref, g1_ref, wr_ref, wg_ref, br_ref, bi_ref, lam_ref, h0_ref,
         pm_ref, pinv_ref, pw_ref, ps_ref, wlo_ref, wpo_ref, wo_ref,
         out_ref, hfin_ref, a_s, b_s, carry_s) = refs
    else:
        (uc_ref, wg_ref, br_ref, bi_ref, lam_ref, h0_ref, hfin_ref, a_s, b_s, carry_s) = refs
    j = pl.program_id(1)

    @pl.when(j == 0)
    def _():
        carry_s[...] = h0_ref[0]

    _gate_coeffs(uc_ref[0], wg_ref, br_ref[...], bi_ref[...], lam_ref[...], a_s, b_s)

    if full:
        def emit(r0, hrows):
            b_s[pl.ds(r0, SUBLANES), :] = hrows + hf_ref[0, pl.ds(r0, SUBLANES), :]
    else:
        def emit(r0, hrows):
            del r0, hrows

    c = _scan_tile(a_s, b_s, emit, carry_s[...], tile, reverse=True)
    carry_s[...] = c
    hfin_ref[0] = c
    if not full:
        return

    mod = mod_ref[0]
    x = x_ref[0]
    hm = _modulate(x, g1_ref[...], mod[1:2], mod[0:1]).astype(BF16)
    o_p, o_t = D_LRU, D_LRU + D_POOL

    g = jnp.dot(hm, wr_ref[:, 0:o_p], preferred_element_type=F32)
    lru_in = (b_s[...] * jax.nn.gelu(g)).astype(BF16)
    lru_out = jnp.dot(lru_in, wlo_ref[...], preferred_element_type=F32)

    p = jnp.dot(hm, wr_ref[:, o_p:o_t], preferred_element_type=F32)
    p_hi = p.astype(BF16)
    p_lo = (p - p_hi.astype(F32)).astype(BF16)
    ys = []
    for gi in range(len(POOL_WINDOWS)):
        cols = slice(gi * POOL_GROUP, (gi + 1) * POOL_GROUP)
        wsum = (jnp.dot(pm_ref[gi], p_hi[:, cols], preferred_element_type=F32)
                + jnp.dot(pm_ref[gi], p_lo[:, cols], preferred_element_type=F32))
        m = wsum * pinv_ref[:, cols] - p[:, cols]
        ys.append(jnp.dot(m.astype(BF16), pw_ref[gi], preferred_element_type=F32))
    y = jnp.concatenate(ys, axis=-1) * ps_ref[...]
    pool_out = jnp.dot(y.astype(BF16), wpo_ref[...], preferred_element_type=F32)

    gt = jnp.dot(hm, wr_ref[:, o_t:], preferred_element_type=F32)
    merged = (jax.nn.sigmoid(gt[:, :D_MODEL]) * lru_out
              + jax.nn.sigmoid(gt[:, D_MODEL:]) * pool_out)
    yo = jnp.dot(merged.astype(BF16), wo_ref[...], preferred_element_type=F32)
    out_ref[0] = x + mod[2:3] * yo


def _mixer_call(x, uc, hf, mod, mod_row, g1, wr, wg, br, bi, lam, h0, pm, pinv, pw, ps,
                wlo, wpo, wo, *, tile, full):
    bsz, seq, d = uc.shape
    n_tiles = seq // tile
    tok = lambda b, j: (b, n_tiles - 1 - j, 0)
    state = pl.BlockSpec((1, 1, D_LRU), lambda b, j: (b, 0, 0))
    tok_spec = pl.BlockSpec((1, tile, d), tok)
    gate_specs = [_const_spec(wg.shape), _const_spec(br.shape), _const_spec(bi.shape),
                  _const_spec(lam.shape), state]
    if full:
        args = (x, uc, hf, mod, g1, wr, wg, br, bi, lam, h0, pm, pinv, pw, ps, wlo, wpo, wo)
        in_specs = ([tok_spec, tok_spec, tok_spec,
                     pl.BlockSpec((1, N_MOD, d), lambda b, j: (mod_row(b), 0, 0)),
                     _const_spec(g1.shape), _const_spec(wr.shape)] + gate_specs
                    + [_const_spec(a.shape) for a in (pm, pinv, pw, ps, wlo, wpo, wo)])
        out_specs = [tok_spec, state]
        out_shape = [jax.ShapeDtypeStruct((bsz, seq, d), F32),
                     jax.ShapeDtypeStruct((bsz, 1, D_LRU), F32)]
    else:
        args = (uc, wg, br, bi, lam, h0)
        in_specs = [tok_spec] + gate_specs
        out_specs = state
        out_shape = jax.ShapeDtypeStruct((bsz, 1, D_LRU), F32)
    return pl.pallas_call(
        functools.partial(_mixer_kernel, tile=tile, full=full),
        grid=(bsz, n_tiles),
        in_specs=in_specs,
        out_specs=out_specs,
        out_shape=out_shape,
        scratch_shapes=[
            pltpu.VMEM((tile, D_LRU), F32),
            pltpu.VMEM((tile, D_LRU), F32),
            pltpu.VMEM((1, D_LRU), F32),
        ],
        compiler_params=pltpu.CompilerParams(
            dimension_semantics=("arbitrary", "arbitrary"), vmem_limit_bytes=VMEM_LIMIT),
        name=f"mixer_t{tile}" if full else f"state_t{tile}",
    )(*args)


def _mlp_kernel(x_ref, mod_ref, g2_ref, w1_ref, w2_ref, fg_ref, out_ref, hm_s, acc_s, *,
                n_chunks, final_norm):
    k = pl.program_id(2)

    @pl.when(k == 0)
    def _():
        mod = mod_ref[0]
        hm_s[...] = _modulate(x_ref[0], g2_ref[...], mod[4:5], mod[3:4]).astype(BF16)
        acc_s[...] = jnp.zeros_like(acc_s)

    hid = jnp.dot(hm_s[...], w1_ref[...], preferred_element_type=F32)
    hid = jnp.square(jnp.maximum(hid, 0.0))
    acc_s[...] += jnp.dot(hid.astype(BF16), w2_ref[...], preferred_element_type=F32)

    @pl.when(k == n_chunks - 1)
    def _():
        y = x_ref[0] + mod_ref[0][5:6] * acc_s[...]
        if final_norm:
            ms = jnp.mean(y * y, axis=-1, keepdims=True)
            y = y * lax.rsqrt(ms + EPS) * fg_ref[...]
        out_ref[0] = y


def _mlp_call(x, mod, mod_row, g2, w1, w2, fg, *, tile, final_norm):
    bsz, seq, d = x.shape
    n_chunks = D_FF // FF_CHUNK
    tok = pl.BlockSpec((1, tile, d), lambda b, j, k: (b, j, 0))
    return pl.pallas_call(
        functools.partial(_mlp_kernel, n_chunks=n_chunks, final_norm=final_norm),
        grid=(bsz, seq // tile, n_chunks),
        in_specs=[
            tok,
            pl.BlockSpec((1, N_MOD, d), lambda b, j, k: (mod_row(b), 0, 0)),
            _const_spec(g2.shape),
            pl.BlockSpec((d, FF_CHUNK), lambda b, j, k: (0, k)),
            pl.BlockSpec((FF_CHUNK, d), lambda b, j, k: (k, 0)),
            _const_spec(fg.shape),
        ],
        out_specs=tok,
        out_shape=jax.ShapeDtypeStruct((bsz, seq, d), F32),
        scratch_shapes=[pltpu.VMEM((tile, d), BF16), pltpu.VMEM((tile, d), F32)],
        compiler_params=pltpu.CompilerParams(
            dimension_semantics=("arbitrary", "arbitrary", "arbitrary"),
            vmem_limit_bytes=VMEM_LIMIT),
        name=f"mlp_t{tile}",
    )(x, mod, g2, w1, w2, fg)


def _pool_tables(tile, row_len):
    t = np.arange(tile)
    pos = t % row_len
    same_row = (t[:, None] // row_len) == (t[None, :] // row_len)
    mats, invs = [], []
    for w in POOL_WINDOWS:
        lo = np.clip(pos - w // 2, 0, row_len)
        hi = np.clip(pos + w - w // 2, 0, row_len)
        member = same_row & (pos[None, :] >= lo[:, None]) & (pos[None, :] < hi[:, None])
        mats.append(member.astype(np.float32))
        invs.append(np.repeat((1.0 / (hi - lo).astype(np.float32))[:, None], POOL_GROUP, axis=1))
    return jnp.asarray(np.stack(mats), BF16), jnp.asarray(np.concatenate(invs, axis=1), F32)


def _gate_weights(w_r, w_i):
    hpg = GATE_GROUP // LRU_HEAD_DIM
    eye = jnp.eye(hpg, dtype=w_r.dtype)

    def blockdiag(w):
        w4 = w.reshape(N_GATE_GROUPS, hpg, LRU_HEAD_DIM, LRU_HEAD_DIM)
        return jnp.einsum('qhde,hk->qhdke', w4, eye).reshape(N_GATE_GROUPS, GATE_GROUP, GATE_GROUP)

    return jnp.concatenate([blockdiag(w_r), blockdiag(w_i)], axis=-1).astype(BF16)


def kernel(x, c, ctx, c_ctx, w_ada, b_ada, norm1_g, norm2_g, w_in, conv_w, conv_b, lru_w_r, lru_b_r,
           lru_w_i, lru_b_i, lru_lambda, w_lru_out, pool_w, pool_scale, w_pool_out, w_o, mlp_w1,
           mlp_w2, final_g):
    bsz, seq, d = x.shape
    ctx_len = ctx.shape[1]
    depth = w_ada.shape[0]
    assert d == D_MODEL and seq % SEQ_TILE == 0 and seq % MLP_TILE == 0 and SEQ_TILE % GRID_W == 0
    assert bsz + 1 <= SUBLANES and ctx_len % SUBLANES == 0

    cc = jnp.concatenate([c, c_ctx[None], jnp.zeros((SUBLANES - bsz - 1, d), F32)], axis=0)
    mods = _ada_call(cc, w_ada, b_ada).reshape(depth, SUBLANES, N_MOD, d)
    lat_row = lambda b: b
    ctx_row = lambda b: bsz

    pm_lat, pinv_lat = _pool_tables(SEQ_TILE, GRID_W)
    pm_ctx, pinv_ctx = _pool_tables(ctx_len, ctx_len)
    zeros_state = jnp.zeros((bsz, 1, D_LRU), F32)
    row = lambda v: v.reshape(1, -1)
    fg = row(final_g)

    for l in range(depth):
        last = l == depth - 1
        mod = mods[l]
        g1, g2 = row(norm1_g[l]), row(norm2_g[l])
        w_in_b = w_in[l].astype(BF16)
        wu, wr = w_in_b[:, :D_LRU], w_in_b[:, D_LRU:]
        cw, cb = conv_w[l], row(conv_b[l])
        wg = [_gate_weights(lru_w_r[l, dr], lru_w_i[l, dr]) for dr in range(2)]
        br = [row(lru_b_r[l, dr]) for dr in range(2)]
        bi = [row(lru_b_i[l, dr]) for dr in range(2)]
        lam = [row(lru_lambda[l, dr]) for dr in range(2)]
        pw, ps = pool_w[l].astype(BF16), row(pool_scale[l])
        wlo, wpo, wo = (w_lru_out[l].astype(BF16), w_pool_out[l].astype(BF16), w_o[l].astype(BF16))
        w1, w2 = mlp_w1[l].astype(BF16), mlp_w2[l].astype(BF16)

        front = functools.partial(_front_call, g1=g1, wu=wu, cw=cw, cb=cb, wg=wg[0], br=br[0],
                                  bi=bi[0], lam=lam[0])
        mixer = functools.partial(_mixer_call, g1=g1, wr=wr, wg=wg[1], br=br[1], bi=bi[1],
                                  lam=lam[1], pw=pw, ps=ps, wlo=wlo, wpo=wpo, wo=wo)

        uc_c, hf_c, hfin_f = front(ctx, mod, ctx_row, h0=zeros_state, tile=ctx_len)
        if last:
            hfin_b = mixer(None, uc_c, None, None, None, h0=zeros_state, pm=None, pinv=None,
                           tile=ctx_len, full=False)
        else:
            ctx, hfin_b = mixer(ctx, uc_c, hf_c, mod, ctx_row, h0=zeros_state, pm=pm_ctx,
                                pinv=pinv_ctx, tile=ctx_len, full=True)
            ctx = _mlp_call(ctx, mod, ctx_row, g2, w1, w2, fg, tile=ctx_len, final_norm=False)

        uc_l, hf_l, _ = front(x, mod, lat_row, h0=hfin_f, tile=SEQ_TILE)
        x, _ = mixer(x, uc_l, hf_l, mod, lat_row, h0=hfin_b, pm=pm_lat, pinv=pinv_lat,
                     tile=SEQ_TILE, full=True)
        x = _mlp_call(x, mod, lat_row, g2, w1, w2, fg, tile=MLP_TILE, final_norm=last)
    return x
```

```python
import functools

import jax
import jax.numpy as jnp
import numpy as np
from jax import lax
from jax.experimental import pallas as pl
from jax.experimental.pallas import tpu as pltpu

D_MODEL = 1024
D_LRU = 1024
LRU_HEADS = 16
LRU_HEAD_DIM = D_LRU // LRU_HEADS
CONV_WIDTH = 4
CONV_LEFT = 2
LRU_C = 8.0
D_POOL = 512
POOL_WINDOWS = (2, 4, 8, 16)
POOL_GROUP = D_POOL // len(POOL_WINDOWS)
D_FF = 4 * D_MODEL
N_MOD = 6
GRID_W = 64
EPS = 1e-6
LOG2_E = 1.4426950408889634

SUBLANES = 8
HALO = 16
GATE_GROUP = 256
N_GATE_GROUPS = D_LRU // GATE_GROUP
POOL_BLOCK = 256
SEQ_TILE = 512
MLP_TILE = 1024
FF_CHUNK = 1024
ADA_CHUNK = 1536
VMEM_LIMIT = 56 * 1024 * 1024

F32 = jnp.float32
BF16 = jnp.bfloat16


def _layer_spec(arr, *lead):
    n_lead = len(lead)
    shape = (1,) * n_lead + arr.shape[n_lead:]
    idx = tuple(lead) + (0,) * (arr.ndim - n_lead)
    return pl.BlockSpec(shape, lambda *_: idx, pipeline_mode=pl.Buffered(1))


def _modulate(x, g, scale, shift):
    ms = jnp.mean(x * x, axis=-1, keepdims=True)
    return (x * lax.rsqrt(ms + EPS) * g) * (1.0 + scale) + shift


def _softplus(x):
    return jnp.maximum(x, 0.0) + jnp.log1p(jnp.exp(-jnp.abs(x)))


def _ada_kernel(c_ref, w_ref, b_ref, o_ref):
    c = c_ref[...]
    s = c * jax.nn.sigmoid(c)
    o_ref[0] = jnp.dot(s, w_ref[0], preferred_element_type=F32,
                       precision=lax.Precision.HIGHEST) + b_ref[0]


def _ada_call(cc, w_ada, b_ada):
    depth, d, n = w_ada.shape
    return pl.pallas_call(
        _ada_kernel,
        grid=(depth, n // ADA_CHUNK),
        in_specs=[
            pl.BlockSpec((SUBLANES, d), lambda l, k: (0, 0)),
            pl.BlockSpec((1, d, ADA_CHUNK), lambda l, k: (l, 0, k)),
            pl.BlockSpec((1, 1, ADA_CHUNK), lambda l, k: (l, 0, k)),
        ],
        out_specs=pl.BlockSpec((1, SUBLANES, ADA_CHUNK), lambda l, k: (l, 0, k)),
        out_shape=jax.ShapeDtypeStruct((depth, SUBLANES, n), F32),
        compiler_params=pltpu.CompilerParams(
            dimension_semantics=("arbitrary", "arbitrary"), vmem_limit_bytes=VMEM_LIMIT),
        name="adaln",
    )(cc, w_ada, b_ada.reshape(depth, 1, n))


def _decay_rates(lam):
    sp = _softplus(-lam)
    return LRU_C * sp, (-LRU_C * LOG2_E) * sp


def _gate_coeffs(uc, wg, br, bi, nla_rate, l2a_rate):
    z = jnp.dot(uc.astype(BF16), wg, preferred_element_type=F32)
    r = jax.nn.sigmoid(z[:, :GATE_GROUP] + br)
    i = jax.nn.sigmoid(z[:, GATE_GROUP:] + bi)
    a = jnp.exp2(r * l2a_rate)
    t = jnp.tanh(r * nla_rate) * (1.0 + a * a)
    mult = jnp.where(t > 0.0, t * lax.rsqrt(t), 0.0)
    return a, mult * (i * uc)


def _scan_cols(a, b, c, emit, reverse):
    n_groups = a.shape[0] // SUBLANES
    sub = lax.broadcasted_iota(jnp.int32, (SUBLANES, a.shape[1]), 0)
    order = range(n_groups - 1, -1, -1) if reverse else range(n_groups)
    for gi in order:
        r0 = gi * SUBLANES
        ag = a[r0:r0 + SUBLANES]
        bg = b[r0:r0 + SUBLANES]
        for k in (1, 2, 4):
            shift = (SUBLANES - k) if reverse else k
            m = (sub < SUBLANES - k) if reverse else (sub >= k)
            ra = pltpu.roll(ag, shift, 0)
            rb = pltpu.roll(bg, shift, 0)
            bg = jnp.where(m, ag * rb, 0.0) + bg
            ag = jnp.where(m, ag * ra, ag)
        h = ag * c + bg
        emit(r0, h)
        c = h[0:1] if reverse else h[SUBLANES - 1:SUBLANES]
    return c


def _group_cols():
    return [slice(q * GATE_GROUP, (q + 1) * GATE_GROUP) for q in range(N_GATE_GROUPS)]


def _fwd_kernel(*refs, tile, n_tiles, full):
    if full:
        (x_ref, xp_ref, xn_ref, mod_ref, g1_ref, win_ref, cw_ref, cb_ref, wg_ref, br_ref, bi_ref,
         lam_ref, h0_ref, pm_ref, pinv_ref, pw_ref, ps_ref, wpo_ref,
         uc_ref, hf_ref, hfin_ref, gg_ref, gl_ref, gp_ref, u_s, carry_s) = refs
    else:
        (x_ref, xp_ref, xn_ref, mod_ref, g1_ref, win_ref, cw_ref, cb_ref, wg_ref, br_ref, bi_ref,
         lam_ref, h0_ref, uc_ref, hfin_ref, u_s, carry_s) = refs
    j = pl.program_id(1)
    nq = N_GATE_GROUPS
    qcols = _group_cols()

    @pl.when(j == 0)
    def _():
        carry_s[...] = h0_ref[0]

    mod = mod_ref[0, 0]
    x_ext = jnp.concatenate([xp_ref[0], x_ref[0], xn_ref[0]], axis=0)
    hb_ext = _modulate(x_ext, g1_ref[0], mod[1:2], mod[0:1]).astype(BF16)
    hb = hb_ext[HALO:HALO + tile]
    row = lax.broadcasted_iota(jnp.int32, (tile + 2 * HALO, 1), 0)
    valid = ((row >= HALO) | (j > 0)) & ((row < tile + HALO) | (j < n_tiles - 1))
    cw = cw_ref[0]
    cb = cb_ref[0]
    nla_rate, l2a_rate = _decay_rates(lam_ref[0, 0])
    br, bi = br_ref[0, 0], bi_ref[0, 0]
    o_g, o_p, o_t = D_LRU, 2 * D_LRU, 2 * D_LRU + D_POOL

    def w_cols(lo, width=GATE_GROUP):
        return win_ref[0, :, lo:lo + width]

    def conv(q):
        u = jnp.dot(hb_ext, w_cols(q * GATE_GROUP), preferred_element_type=F32)
        u_s[:, qcols[q]] = jnp.where(valid, u, 0.0)
        uc = cb[:, qcols[q]] + cw[0:1, qcols[q]] * u_s[pl.ds(HALO - CONV_LEFT, tile), qcols[q]]
        for k in range(1, CONV_WIDTH):
            uc = uc + cw[k:k + 1, qcols[q]] * u_s[pl.ds(HALO - CONV_LEFT + k, tile), qcols[q]]
        uc_ref[0, :, qcols[q]] = uc
        return uc

    def coeffs(q, uc):
        return _gate_coeffs(uc, wg_ref[0, 0, q], br[:, qcols[q]], bi[:, qcols[q]],
                            nla_rate[:, qcols[q]], l2a_rate[:, qcols[q]])

    def scan(q, ab):
        if full:
            def emit(r0, h):
                hf_ref[0, r0:r0 + SUBLANES, qcols[q]] = h
        else:
            def emit(r0, h):
                del r0, h
        c = _scan_cols(ab[0], ab[1], carry_s[:, qcols[q]], emit, reverse=False)
        carry_s[:, qcols[q]] = c
        hfin_ref[0, :, qcols[q]] = c

    if not full:
        for q in range(nq):
            scan(q, coeffs(q, conv(q)))
        return

    def pool_group(gi, p, p_hi, p_lo):
        cols = slice(gi * POOL_GROUP, (gi + 1) * POOL_GROUP)
        wsum = []
        for r0 in range(0, tile, POOL_BLOCK):
            rws = slice(r0, r0 + POOL_BLOCK)
            rhs = jnp.concatenate([p_hi[rws, cols], p_lo[rws, cols]], axis=1)
            s2 = jnp.dot(pm_ref[gi], rhs, preferred_element_type=F32)
            wsum.append(s2[:, :POOL_GROUP] + s2[:, POOL_GROUP:])
        m = jnp.concatenate(wsum, axis=0) * pinv_ref[:, cols] - p[:, cols]
        return jnp.dot(m.astype(BF16), pw_ref[0, gi], preferred_element_type=F32)

    def gelu_cols(q):
        g = jnp.dot(hb, w_cols(o_g + q * GATE_GROUP), preferred_element_type=F32)
        gg_ref[0, :, qcols[q]] = jax.nn.gelu(g).astype(BF16)

    def lru_gate_cols(q):
        gt = jnp.dot(hb, w_cols(o_t + q * GATE_GROUP), preferred_element_type=F32)
        gl_ref[0, :, qcols[q]] = jax.nn.sigmoid(gt).astype(BF16)

    def pool_gate_cols(q, yb):
        gt = jnp.dot(hb, w_cols(o_t + D_MODEL + q * GATE_GROUP), preferred_element_type=F32)
        po = jnp.dot(yb, wpo_ref[0, :, qcols[q]], preferred_element_type=F32)
        gp_ref[0, :, qcols[q]] = (jax.nn.sigmoid(gt) * po).astype(BF16)

    ucs = [conv(0), conv(1)]
    p = jnp.dot(hb, w_cols(o_p, D_POOL), preferred_element_type=F32)
    p_hi = p.astype(BF16)
    p_lo = (p - p_hi.astype(F32)).astype(BF16)
    ucs.append(conv(2))
    ys = [pool_group(0, p, p_hi, p_lo), pool_group(1, p, p_hi, p_lo)]
    ucs.append(conv(3))
    ab = coeffs(0, ucs[0])
    ys += [pool_group(2, p, p_hi, p_lo), pool_group(3, p, p_hi, p_lo)]
    yb = (jnp.concatenate(ys, axis=-1) * ps_ref[0]).astype(BF16)
    for q in range(nq):
        gelu_cols(q)
        nxt = coeffs(q + 1, ucs[q + 1]) if q + 1 < nq else None
        lru_gate_cols(q)
        scan(q, ab)
        pool_gate_cols(q, yb)
        ab = nxt


def _fwd_call(x, mods, mod_row, h0, P, l, pm, pinv, *, tile, full):
    bsz, seq, d = x.shape
    n_tiles = seq // tile
    hb = tile // HALO
    n_hb = seq // HALO
    tok = lambda b, j: (b, j, 0)
    tok_spec = pl.BlockSpec((1, tile, d), tok)
    state = pl.BlockSpec((1, 1, D_LRU), lambda b, j: (b, 0, 0))
    args = [x, x, x, mods, P['g1'], P['w_in'], P['cw'], P['cb'], P['wg'], P['br'], P['bi'],
            P['lam'], h0]
    in_specs = [
        tok_spec,
        pl.BlockSpec((1, HALO, d), lambda b, j: (b, jnp.maximum(j * hb - 1, 0), 0)),
        pl.BlockSpec((1, HALO, d), lambda b, j: (b, jnp.minimum((j + 1) * hb, n_hb - 1), 0)),
        pl.BlockSpec((1, 1, N_MOD, d), lambda b, j: (l, mod_row(b), 0, 0)),
        _layer_spec(P['g1'], l), _layer_spec(P['w_in'], l), _layer_spec(P['cw'], l),
        _layer_spec(P['cb'], l), _layer_spec(P['wg'], l, 0), _layer_spec(P['br'], l, 0),
        _layer_spec(P['bi'], l, 0), _layer_spec(P['lam'], l, 0), state,
    ]
    f32_tok = jax.ShapeDtypeStruct((bsz, seq, D_LRU), F32)
    bf_tok = jax.ShapeDtypeStruct((bsz, seq, D_LRU), BF16)
    st = jax.ShapeDtypeStruct((bsz, 1, D_LRU), F32)
    if full:
        args += [pm, pinv, P['pw'], P['ps'], P['wpo']]
        in_specs += [_layer_spec(pm), _layer_spec(pinv), _layer_spec(P['pw'], l),
                     _layer_spec(P['ps'], l), _layer_spec(P['wpo'], l)]
        out_specs = [tok_spec, tok_spec, state, tok_spec, tok_spec, tok_spec]
        out_shape = [f32_tok, f32_tok, st, bf_tok, bf_tok, bf_tok]
    else:
        out_specs = [tok_spec, state]
        out_shape = [f32_tok, st]
    return pl.pallas_call(
        functools.partial(_fwd_kernel, tile=tile, n_tiles=n_tiles, full=full),
        grid=(bsz, n_tiles),
        in_specs=in_specs,
        out_specs=out_specs,
        out_shape=out_shape,
        scratch_shapes=[
            pltpu.VMEM((tile + 2 * HALO, D_LRU), F32),
            pltpu.VMEM((1, D_LRU), F32),
        ],
        compiler_params=pltpu.CompilerParams(
            dimension_semantics=("arbitrary", "arbitrary"), vmem_limit_bytes=VMEM_LIMIT),
        name=f"fwd_t{tile}" if full else f"fwdstate_t{tile}",
    )(*args)


def _bwd_kernel(*refs, tile, full):
    if full:
        (uc_ref, wg_ref, br_ref, bi_ref, lam_ref, h0_ref, x_ref, hf_ref, gg_ref, gl_ref, gp_ref,
         mod_ref, wlo_ref, wo_ref, hfin_ref, out_ref, hl_s, carry_s) = refs
    else:
        (uc_ref, wg_ref, br_ref, bi_ref, lam_ref, h0_ref, hfin_ref, carry_s) = refs
    j = pl.program_id(1)
    nq = N_GATE_GROUPS
    qcols = _group_cols()

    @pl.when(j == 0)
    def _():
        carry_s[...] = h0_ref[0]

    nla_rate, l2a_rate = _decay_rates(lam_ref[0, 0])
    br, bi = br_ref[0, 0], bi_ref[0, 0]

    def coeffs(q):
        return _gate_coeffs(uc_ref[0, :, qcols[q]], wg_ref[0, 0, q], br[:, qcols[q]],
                            bi[:, qcols[q]], nla_rate[:, qcols[q]], l2a_rate[:, qcols[q]])

    def scan(q, ab):
        if full:
            def emit(r0, h):
                rws = slice(r0, r0 + SUBLANES)
                hl_s[rws, qcols[q]] = h + hf_ref[0, rws, qcols[q]]
        else:
            def emit(r0, h):
                del r0, h
        c = _scan_cols(ab[0], ab[1], carry_s[:, qcols[q]], emit, reverse=True)
        carry_s[:, qcols[q]] = c
        hfin_ref[0, :, qcols[q]] = c

    ab = coeffs(0)
    lru_out = None
    for q in range(nq):
        nxt = coeffs(q + 1) if q + 1 < nq else None
        scan(q, ab)
        ab = nxt
        if full:
            lru_in = (hl_s[:, qcols[q]] * gg_ref[0, :, qcols[q]].astype(F32)).astype(BF16)
            part = jnp.dot(lru_in, wlo_ref[0, qcols[q], :], preferred_element_type=F32)
            lru_out = part if lru_out is None else lru_out + part
    if not full:
        return

    merged = gl_ref[0].astype(F32) * lru_out + gp_ref[0].astype(F32)
    yo = jnp.dot(merged.astype(BF16), wo_ref[0], preferred_element_type=F32)
    out_ref[0] = x_ref[0] + mod_ref[0, 0][2:3] * yo


def _bwd_call(uc, h0, P, l, x=None, hf=None, gg=None, gl=None, gp=None, mods=None, mod_row=None,
              *, tile, full):
    bsz, seq, d = uc.shape
    n_tiles = seq // tile
    tok = lambda b, j: (b, n_tiles - 1 - j, 0)
    tok_spec = pl.BlockSpec((1, tile, d), tok)
    state = pl.BlockSpec((1, 1, D_LRU), lambda b, j: (b, 0, 0))
    args = [uc, P['wg'], P['br'], P['bi'], P['lam'], h0]
    in_specs = [tok_spec, _layer_spec(P['wg'], l, 1), _layer_spec(P['br'], l, 1),
                _layer_spec(P['bi'], l, 1), _layer_spec(P['lam'], l, 1), state]
    st = jax.ShapeDtypeStruct((bsz, 1, D_LRU), F32)
    scratch = [pltpu.VMEM((1, D_LRU), F32)]
    if full:
        args += [x, hf, gg, gl, gp, mods, P['wlo'], P['wo']]
        in_specs += [tok_spec] * 5 + [
            pl.BlockSpec((1, 1, N_MOD, d), lambda b, j: (l, mod_row(b), 0, 0)),
            _layer_spec(P['wlo'], l), _layer_spec(P['wo'], l)]
        out_specs = [state, tok_spec]
        out_shape = [st, jax.ShapeDtypeStruct((bsz, seq, d), F32)]
        scratch = [pltpu.VMEM((tile, D_LRU), F32)] + scratch
    else:
        out_specs = state
        out_shape = st
    return pl.pallas_call(
        functools.partial(_bwd_kernel, tile=tile, full=full),
        grid=(bsz, n_tiles),
        in_specs=in_specs,
        out_specs=out_specs,
        out_shape=out_shape,
        scratch_shapes=scratch,
        compiler_params=pltpu.CompilerParams(
            dimension_semantics=("arbitrary", "arbitrary"), vmem_limit_bytes=VMEM_LIMIT),
        name=f"bwd_t{tile}" if full else f"bwdstate_t{tile}",
    )(*args)


def _mlp_kernel(x_ref, mod_ref, g2_ref, w1_ref, w2_ref, fg_ref, out_ref, hm_s, acc_s, *,
                n_chunks, final_norm):
    k = pl.program_id(2)

    @pl.when(k == 0)
    def _():
        mod = mod_ref[0, 0]
        hm_s[...] = _modulate(x_ref[0], g2_ref[0], mod[4:5], mod[3:4]).astype(BF16)
        acc_s[...] = jnp.zeros_like(acc_s)

    hid = jnp.dot(hm_s[...], w1_ref[0], preferred_element_type=F32)
    hid = jnp.square(jnp.maximum(hid, 0.0))
    acc_s[...] += jnp.dot(hid.astype(BF16), w2_ref[0], preferred_element_type=F32)

    @pl.when(k == n_chunks - 1)
    def _():
        y = x_ref[0] + mod_ref[0, 0][5:6] * acc_s[...]
        if final_norm:
            ms = jnp.mean(y * y, axis=-1, keepdims=True)
            y = y * lax.rsqrt(ms + EPS) * fg_ref[...]
        out_ref[0] = y


def _mlp_call(x, mods, mod_row, P, l, fg, *, tile, final_norm):
    bsz, seq, d = x.shape
    n_chunks = D_FF // FF_CHUNK
    tok = pl.BlockSpec((1, tile, d), lambda b, j, k: (b, j, 0))
    return pl.pallas_call(
        functools.partial(_mlp_kernel, n_chunks=n_chunks, final_norm=final_norm),
        grid=(bsz, seq // tile, n_chunks),
        in_specs=[
            tok,
            pl.BlockSpec((1, 1, N_MOD, d), lambda b, j, k: (l, mod_row(b), 0, 0)),
            _layer_spec(P['g2'], l),
            pl.BlockSpec((1, d, FF_CHUNK), lambda b, j, k: (l, 0, k)),
            pl.BlockSpec((1, FF_CHUNK, d), lambda b, j, k: (l, k, 0)),
            _layer_spec(fg),
        ],
        out_specs=tok,
        out_shape=jax.ShapeDtypeStruct((bsz, seq, d), F32),
        scratch_shapes=[pltpu.VMEM((tile, d), BF16), pltpu.VMEM((tile, d), F32)],
        compiler_params=pltpu.CompilerParams(
            dimension_semantics=("arbitrary", "arbitrary", "arbitrary"),
            vmem_limit_bytes=VMEM_LIMIT),
        name=f"mlp_t{tile}",
    )(x, mods, P['g2'], P['w1'], P['w2'], fg)


def _pool_tables(tile, row_len):
    assert POOL_BLOCK % row_len == 0 and tile % POOL_BLOCK == 0
    t = np.arange(POOL_BLOCK)
    pos = t % row_len
    same_row = (t[:, None] // row_len) == (t[None, :] // row_len)
    mats, invs = [], []
    for w in POOL_WINDOWS:
        lo = np.clip(pos - w // 2, 0, row_len)
        hi = np.clip(pos + w - w // 2, 0, row_len)
        member = same_row & (pos[None, :] >= lo[:, None]) & (pos[None, :] < hi[:, None])
        mats.append(member.astype(np.float32))
        inv = np.repeat((1.0 / (hi - lo).astype(np.float32))[:, None], POOL_GROUP, axis=1)
        invs.append(np.tile(inv, (tile // POOL_BLOCK, 1)))
    return jnp.asarray(np.stack(mats), BF16), jnp.asarray(np.concatenate(invs, axis=1), F32)


def _gate_weights(w_r, w_i):
    hpg = GATE_GROUP // LRU_HEAD_DIM
    eye = jnp.eye(hpg, dtype=BF16)

    def blockdiag(w):
        w6 = w.astype(BF16).reshape(w.shape[0], 2, N_GATE_GROUPS, hpg, LRU_HEAD_DIM, LRU_HEAD_DIM)
        bd = jnp.einsum('ldqhie,hk->ldqhike', w6, eye)
        return bd.reshape(w.shape[0], 2, N_GATE_GROUPS, GATE_GROUP, GATE_GROUP)

    return jnp.concatenate([blockdiag(w_r), blockdiag(w_i)], axis=-1)


def kernel(x, c, ctx, c_ctx, w_ada, b_ada, norm1_g, norm2_g, w_in, conv_w, conv_b, lru_w_r, lru_b_r,
           lru_w_i, lru_b_i, lru_lambda, w_lru_out, pool_w, pool_scale, w_pool_out, w_o, mlp_w1,
           mlp_w2, final_g):
    bsz, seq, d = x.shape
    ctx_len = ctx.shape[1]
    depth = w_ada.shape[0]
    assert d == D_MODEL and seq % SEQ_TILE == 0 and seq % MLP_TILE == 0
    assert bsz + 1 <= SUBLANES and ctx_len % HALO == 0

    cc = jnp.concatenate([c, c_ctx[None], jnp.zeros((SUBLANES - bsz - 1, d), F32)], axis=0)
    mods = _ada_call(cc, w_ada, b_ada).reshape(depth, SUBLANES, N_MOD, d)
    lat_row = lambda b: b
    ctx_row = lambda b: bsz

    pm_lat, pinv_lat = _pool_tables(SEQ_TILE, GRID_W)
    pm_ctx, pinv_ctx = _pool_tables(ctx_len, ctx_len)
    zeros_state = jnp.zeros((bsz, 1, D_LRU), F32)
    fg = final_g.reshape(1, d)
    rows = lambda v: v.reshape(v.shape[:-1] + (1, v.shape[-1]))
    P = dict(
        g1=rows(norm1_g), g2=rows(norm2_g), w_in=w_in.astype(BF16), cw=conv_w, cb=rows(conv_b),
        wg=_gate_weights(lru_w_r, lru_w_i), br=rows(lru_b_r), bi=rows(lru_b_i),
        lam=rows(lru_lambda), pw=pool_w.astype(BF16), ps=rows(pool_scale),
        wpo=w_pool_out.astype(BF16), wlo=w_lru_out.astype(BF16), wo=w_o.astype(BF16),
        w1=mlp_w1.astype(BF16), w2=mlp_w2.astype(BF16))

    for l in range(depth):
        last = l == depth - 1
        if last:
            uc_c, hfin_f = _fwd_call(ctx, mods, ctx_row, zeros_state, P, l, None, None,
                                     tile=ctx_len, full=False)
            hfin_b = _bwd_call(uc_c, zeros_state, P, l, tile=ctx_len, full=False)
        else:
            uc_c, hf_c, hfin_f, gg_c, gl_c, gp_c = _fwd_call(
                ctx, mods, ctx_row, zeros_state, P, l, pm_ctx, pinv_ctx, tile=ctx_len, full=True)
            hfin_b, ctx = _bwd_call(uc_c, zeros_state, P, l, ctx, hf_c, gg_c, gl_c, gp_c, mods,
                                    ctx_row, tile=ctx_len, full=True)
            ctx = _mlp_call(ctx, mods, ctx_row, P, l, fg, tile=ctx_len, final_norm=False)

        uc_l, hf_l, _, gg_l, gl_l, gp_l = _fwd_call(
            x, mods, lat_row, hfin_f, P, l, pm_lat, pinv_lat, tile=SEQ_TILE, full=True)
        _, x = _bwd_call(uc_l, hfin_b, P, l, x, hf_l, gg_l, gl_l, gp_l, mods, lat_row,
                         tile=SEQ_TILE, full=True)
        x = _mlp_call(x, mods, lat_row, P, l, fg, tile=MLP_TILE, final_norm=last)
    return x
```

```python
import functools

import jax
import jax.numpy as jnp
import numpy as np
from jax import lax
from jax.experimental import pallas as pl
from jax.experimental.pallas import tpu as pltpu

D_MODEL = 1024
D_LRU = 1024
LRU_HEADS = 16
LRU_HEAD_DIM = D_LRU // LRU_HEADS
CONV_WIDTH = 4
CONV_LEFT = 2
LRU_C = 8.0
D_POOL = 512
POOL_WINDOWS = (2, 4, 8, 16)
POOL_GROUP = D_POOL // len(POOL_WINDOWS)
D_FF = 4 * D_MODEL
N_MOD = 6
GRID_W = 64
EPS = 1e-6
LOG2_E = 1.4426950408889634

SUBLANES = 8
HALO = 16
GATE_GROUP = 256
N_GATE_GROUPS = D_LRU // GATE_GROUP
POOL_BLOCK = 256
SEQ_TILE = 512
FF_CHUNK = 1024
MLP_FUSED_CHUNK = 512
ADA_CHUNK = 1536
VMEM_LIMIT = 56 * 1024 * 1024

F32 = jnp.float32
BF16 = jnp.bfloat16


def _layer_spec(arr, *lead):
    n_lead = len(lead)
    shape = (1,) * n_lead + arr.shape[n_lead:]
    idx = tuple(lead) + (0,) * (arr.ndim - n_lead)
    return pl.BlockSpec(shape, lambda *_: idx, pipeline_mode=pl.Buffered(1))


def _modulate(x, g, scale, shift):
    ms = jnp.mean(x * x, axis=-1, keepdims=True)
    return (x * lax.rsqrt(ms + EPS) * g) * (1.0 + scale) + shift


def _softplus(x):
    return jnp.maximum(x, 0.0) + jnp.log1p(jnp.exp(-jnp.abs(x)))


def _ada_kernel(c_ref, w_ref, b_ref, o_ref):
    c = c_ref[...]
    s = c * jax.nn.sigmoid(c)
    o_ref[0] = jnp.dot(s, w_ref[0], preferred_element_type=F32,
                       precision=lax.Precision.HIGHEST) + b_ref[0]


def _ada_call(cc, w_ada, b_ada):
    depth, d, n = w_ada.shape
    return pl.pallas_call(
        _ada_kernel,
        grid=(depth, n // ADA_CHUNK),
        in_specs=[
            pl.BlockSpec((SUBLANES, d), lambda l, k: (0, 0)),
            pl.BlockSpec((1, d, ADA_CHUNK), lambda l, k: (l, 0, k)),
            pl.BlockSpec((1, 1, ADA_CHUNK), lambda l, k: (l, 0, k)),
        ],
        out_specs=pl.BlockSpec((1, SUBLANES, ADA_CHUNK), lambda l, k: (l, 0, k)),
        out_shape=jax.ShapeDtypeStruct((depth, SUBLANES, n), F32),
        compiler_params=pltpu.CompilerParams(
            dimension_semantics=("arbitrary", "arbitrary"), vmem_limit_bytes=VMEM_LIMIT),
        name="adaln",
    )(cc, w_ada, b_ada.reshape(depth, 1, n))


def _decay_rates(lam):
    sp = _softplus(-lam)
    return LRU_C * sp, (-LRU_C * LOG2_E) * sp


def _gate_coeffs(uc, wg, br, bi, nla_rate, l2a_rate):
    z = jnp.dot(uc.astype(BF16), wg, preferred_element_type=F32)
    r = jax.nn.sigmoid(z[:, :GATE_GROUP] + br)
    i = jax.nn.sigmoid(z[:, GATE_GROUP:] + bi)
    a = jnp.exp2(r * l2a_rate)
    t = jnp.tanh(r * nla_rate) * (1.0 + a * a)
    mult = jnp.where(t > 0.0, t * lax.rsqrt(t), 0.0)
    return a, mult * (i * uc)


def _scan_cols(a, b, c, emit, reverse):
    n_groups = a.shape[0] // SUBLANES
    sub = lax.broadcasted_iota(jnp.int32, (SUBLANES, a.shape[1]), 0)
    order = range(n_groups - 1, -1, -1) if reverse else range(n_groups)
    for gi in order:
        r0 = gi * SUBLANES
        ag = a[r0:r0 + SUBLANES]
        bg = b[r0:r0 + SUBLANES]
        for k in (1, 2, 4):
            shift = (SUBLANES - k) if reverse else k
            m = (sub < SUBLANES - k) if reverse else (sub >= k)
            ra = pltpu.roll(ag, shift, 0)
            rb = pltpu.roll(bg, shift, 0)
            bg = jnp.where(m, ag * rb, 0.0) + bg
            ag = jnp.where(m, ag * ra, ag)
        h = ag * c + bg
        emit(r0, h)
        c = h[0:1] if reverse else h[SUBLANES - 1:SUBLANES]
    return c


def _group_cols():
    return [slice(q * GATE_GROUP, (q + 1) * GATE_GROUP) for q in range(N_GATE_GROUPS)]


def _fwd_kernel(*refs, tile, n_tiles, full):
    if full:
        (x_ref, xp_ref, xn_ref, mod_ref, g1_ref, win_ref, cw_ref, cb_ref, wg_ref, br_ref, bi_ref,
         lam_ref, h0_ref, pm_ref, pinv_ref, pw_ref, ps_ref, wpo_ref,
         uc_ref, hf_ref, hfin_ref, gg_ref, gl_ref, gp_ref, u_s, carry_s) = refs
    else:
        (x_ref, xp_ref, xn_ref, mod_ref, g1_ref, win_ref, cw_ref, cb_ref, wg_ref, br_ref, bi_ref,
         lam_ref, h0_ref, uc_ref, hfin_ref, u_s, carry_s) = refs
    j = pl.program_id(1)
    nq = N_GATE_GROUPS
    qcols = _group_cols()

    @pl.when(j == 0)
    def _():
        carry_s[...] = h0_ref[0]

    mod = mod_ref[0, 0]
    x_ext = jnp.concatenate([xp_ref[0], x_ref[0], xn_ref[0]], axis=0)
    hb_ext = _modulate(x_ext, g1_ref[0], mod[1:2], mod[0:1]).astype(BF16)
    hb = hb_ext[HALO:HALO + tile]
    row = lax.broadcasted_iota(jnp.int32, (tile + 2 * HALO, 1), 0)
    valid = ((row >= HALO) | (j > 0)) & ((row < tile + HALO) | (j < n_tiles - 1))
    cw = cw_ref[0]
    cb = cb_ref[0]
    nla_rate, l2a_rate = _decay_rates(lam_ref[0, 0])
    br, bi = br_ref[0, 0], bi_ref[0, 0]
    o_g, o_p, o_t = D_LRU, 2 * D_LRU, 2 * D_LRU + D_POOL

    def w_cols(lo, width=GATE_GROUP):
        return win_ref[0, :, lo:lo + width]

    def conv(q):
        u = jnp.dot(hb_ext, w_cols(q * GATE_GROUP), preferred_element_type=F32)
        u_s[:, qcols[q]] = jnp.where(valid, u, 0.0)
        uc = cb[:, qcols[q]] + cw[0:1, qcols[q]] * u_s[pl.ds(HALO - CONV_LEFT, tile), qcols[q]]
        for k in range(1, CONV_WIDTH):
            uc = uc + cw[k:k + 1, qcols[q]] * u_s[pl.ds(HALO - CONV_LEFT + k, tile), qcols[q]]
        uc_ref[0, :, qcols[q]] = uc
        return uc

    def coeffs(q, uc):
        return _gate_coeffs(uc, wg_ref[0, 0, q], br[:, qcols[q]], bi[:, qcols[q]],
                            nla_rate[:, qcols[q]], l2a_rate[:, qcols[q]])

    def scan(q, ab):
        if full:
            def emit(r0, h):
                hf_ref[0, r0:r0 + SUBLANES, qcols[q]] = h
        else:
            def emit(r0, h):
                del r0, h
        c = _scan_cols(ab[0], ab[1], carry_s[:, qcols[q]], emit, reverse=False)
        carry_s[:, qcols[q]] = c
        hfin_ref[0, :, qcols[q]] = c

    if not full:
        for q in range(nq):
            scan(q, coeffs(q, conv(q)))
        return

    def pool_group(gi, p, p_hi, p_lo):
        cols = slice(gi * POOL_GROUP, (gi + 1) * POOL_GROUP)
        wsum = []
        for r0 in range(0, tile, POOL_BLOCK):
            rws = slice(r0, r0 + POOL_BLOCK)
            rhs = jnp.concatenate([p_hi[rws, cols], p_lo[rws, cols]], axis=1)
            s2 = jnp.dot(pm_ref[gi], rhs, preferred_element_type=F32)
            wsum.append(s2[:, :POOL_GROUP] + s2[:, POOL_GROUP:])
        m = jnp.concatenate(wsum, axis=0) * pinv_ref[:, cols] - p[:, cols]
        return jnp.dot(m.astype(BF16), pw_ref[0, gi], preferred_element_type=F32)

    def gelu_cols(q):
        g = jnp.dot(hb, w_cols(o_g + q * GATE_GROUP), preferred_element_type=F32)
        gg_ref[0, :, qcols[q]] = jax.nn.gelu(g).astype(BF16)

    def lru_gate_cols(q):
        gt = jnp.dot(hb, w_cols(o_t + q * GATE_GROUP), preferred_element_type=F32)
        gl_ref[0, :, qcols[q]] = jax.nn.sigmoid(gt).astype(BF16)

    def pool_gate_cols(q, yb):
        gt = jnp.dot(hb, w_cols(o_t + D_MODEL + q * GATE_GROUP), preferred_element_type=F32)
        po = jnp.dot(yb, wpo_ref[0, :, qcols[q]], preferred_element_type=F32)
        gp_ref[0, :, qcols[q]] = (jax.nn.sigmoid(gt) * po).astype(BF16)

    ucs = [conv(0), conv(1)]
    p = jnp.dot(hb, w_cols(o_p, D_POOL), preferred_element_type=F32)
    p_hi = p.astype(BF16)
    p_lo = (p - p_hi.astype(F32)).astype(BF16)
    ucs.append(conv(2))
    ys = [pool_group(0, p, p_hi, p_lo), pool_group(1, p, p_hi, p_lo)]
    ucs.append(conv(3))
    ab = coeffs(0, ucs[0])
    ys += [pool_group(2, p, p_hi, p_lo), pool_group(3, p, p_hi, p_lo)]
    yb = (jnp.concatenate(ys, axis=-1) * ps_ref[0]).astype(BF16)
    for q in range(nq):
        gelu_cols(q)
        nxt = coeffs(q + 1, ucs[q + 1]) if q + 1 < nq else None
        lru_gate_cols(q)
        scan(q, ab)
        pool_gate_cols(q, yb)
        ab = nxt


def _fwd_call(x, mods, mod_row, h0, P, l, pm, pinv, *, tile, full):
    bsz, seq, d = x.shape
    n_tiles = seq // tile
    hb = tile // HALO
    n_hb = seq // HALO
    tok = lambda b, j: (b, j, 0)
    tok_spec = pl.BlockSpec((1, tile, d), tok)
    state = pl.BlockSpec((1, 1, D_LRU), lambda b, j: (b, 0, 0))
    args = [x, x, x, mods, P['g1'], P['w_in'], P['cw'], P['cb'], P['wg'], P['br'], P['bi'],
            P['lam'], h0]
    in_specs = [
        tok_spec,
        pl.BlockSpec((1, HALO, d), lambda b, j: (b, jnp.maximum(j * hb - 1, 0), 0)),
        pl.BlockSpec((1, HALO, d), lambda b, j: (b, jnp.minimum((j + 1) * hb, n_hb - 1), 0)),
        pl.BlockSpec((1, 1, N_MOD, d), lambda b, j: (l, mod_row(b), 0, 0)),
        _layer_spec(P['g1'], l), _layer_spec(P['w_in'], l), _layer_spec(P['cw'], l),
        _layer_spec(P['cb'], l), _layer_spec(P['wg'], l, 0), _layer_spec(P['br'], l, 0),
        _layer_spec(P['bi'], l, 0), _layer_spec(P['lam'], l, 0), state,
    ]
    f32_tok = jax.ShapeDtypeStruct((bsz, seq, D_LRU), F32)
    bf_tok = jax.ShapeDtypeStruct((bsz, seq, D_LRU), BF16)
    st = jax.ShapeDtypeStruct((bsz, 1, D_LRU), F32)
    if full:
        args += [pm, pinv, P['pw'], P['ps'], P['wpo']]
        in_specs += [_layer_spec(pm), _layer_spec(pinv), _layer_spec(P['pw'], l),
                     _layer_spec(P['ps'], l), _layer_spec(P['wpo'], l)]
        out_specs = [tok_spec, tok_spec, state, tok_spec, tok_spec, tok_spec]
        out_shape = [f32_tok, f32_tok, st, bf_tok, bf_tok, bf_tok]
    else:
        out_specs = [tok_spec, state]
        out_shape = [f32_tok, st]
    return pl.pallas_call(
        functools.partial(_fwd_kernel, tile=tile, n_tiles=n_tiles, full=full),
        grid=(bsz, n_tiles),
        in_specs=in_specs,
        out_specs=out_specs,
        out_shape=out_shape,
        scratch_shapes=[
            pltpu.VMEM((tile + 2 * HALO, D_LRU), F32),
            pltpu.VMEM((1, D_LRU), F32),
        ],
        compiler_params=pltpu.CompilerParams(
            dimension_semantics=("arbitrary", "arbitrary"), vmem_limit_bytes=VMEM_LIMIT),
        name=f"fwd_t{tile}" if full else f"fwdstate_t{tile}",
    )(*args)


def _bwd_kernel(*refs, tile, full):
    if full:
        (uc_ref, wg_ref, br_ref, bi_ref, lam_ref, h0_ref, x_ref, hf_ref, gg_ref, gl_ref, gp_ref,
         mod_ref, wlo_ref, wo_ref, hfin_ref, out_ref, hl_s, carry_s) = refs
    else:
        (uc_ref, wg_ref, br_ref, bi_ref, lam_ref, h0_ref, hfin_ref, carry_s) = refs
    j = pl.program_id(1)
    nq = N_GATE_GROUPS
    qcols = _group_cols()

    @pl.when(j == 0)
    def _():
        carry_s[...] = h0_ref[0]

    nla_rate, l2a_rate = _decay_rates(lam_ref[0, 0])
    br, bi = br_ref[0, 0], bi_ref[0, 0]

    def coeffs(q):
        return _gate_coeffs(uc_ref[0, :, qcols[q]], wg_ref[0, 0, q], br[:, qcols[q]],
                            bi[:, qcols[q]], nla_rate[:, qcols[q]], l2a_rate[:, qcols[q]])

    def scan(q, ab):
        if full:
            def emit(r0, h):
                rws = slice(r0, r0 + SUBLANES)
                hl_s[rws, qcols[q]] = h + hf_ref[0, rws, qcols[q]]
        else:
            def emit(r0, h):
                del r0, h
        c = _scan_cols(ab[0], ab[1], carry_s[:, qcols[q]], emit, reverse=True)
        carry_s[:, qcols[q]] = c
        hfin_ref[0, :, qcols[q]] = c

    ab = coeffs(0)
    lru_out = None
    for q in range(nq):
        nxt = coeffs(q + 1) if q + 1 < nq else None
        scan(q, ab)
        ab = nxt
        if full:
            lru_in = (hl_s[:, qcols[q]] * gg_ref[0, :, qcols[q]].astype(F32)).astype(BF16)
            part = jnp.dot(lru_in, wlo_ref[0, qcols[q], :], preferred_element_type=F32)
            lru_out = part if lru_out is None else lru_out + part
    if not full:
        return

    merged = gl_ref[0].astype(F32) * lru_out + gp_ref[0].astype(F32)
    yo = jnp.dot(merged.astype(BF16), wo_ref[0], preferred_element_type=F32)
    out_ref[0] = x_ref[0] + mod_ref[0, 0][2:3] * yo


def _bwd_call(uc, h0, P, l, x=None, hf=None, gg=None, gl=None, gp=None, mods=None, mod_row=None,
              *, tile, full):
    bsz, seq, d = uc.shape
    n_tiles = seq // tile
    tok = lambda b, j: (b, n_tiles - 1 - j, 0)
    tok_spec = pl.BlockSpec((1, tile, d), tok)
    state = pl.BlockSpec((1, 1, D_LRU), lambda b, j: (b, 0, 0))
    args = [uc, P['wg'], P['br'], P['bi'], P['lam'], h0]
    in_specs = [tok_spec, _layer_spec(P['wg'], l, 1), _layer_spec(P['br'], l, 1),
                _layer_spec(P['bi'], l, 1), _layer_spec(P['lam'], l, 1), state]
    st = jax.ShapeDtypeStruct((bsz, 1, D_LRU), F32)
    scratch = [pltpu.VMEM((1, D_LRU), F32)]
    if full:
        args += [x, hf, gg, gl, gp, mods, P['wlo'], P['wo']]
        in_specs += [tok_spec] * 5 + [
            pl.BlockSpec((1, 1, N_MOD, d), lambda b, j: (l, mod_row(b), 0, 0)),
            _layer_spec(P['wlo'], l), _layer_spec(P['wo'], l)]
        out_specs = [state, tok_spec]
        out_shape = [st, jax.ShapeDtypeStruct((bsz, seq, d), F32)]
        scratch = [pltpu.VMEM((tile, D_LRU), F32)] + scratch
    else:
        out_specs = state
        out_shape = st
    return pl.pallas_call(
        functools.partial(_bwd_kernel, tile=tile, full=full),
        grid=(bsz, n_tiles),
        in_specs=in_specs,
        out_specs=out_specs,
        out_shape=out_shape,
        scratch_shapes=scratch,
        compiler_params=pltpu.CompilerParams(
            dimension_semantics=("arbitrary", "arbitrary"), vmem_limit_bytes=VMEM_LIMIT),
        name=f"bwd_t{tile}" if full else f"bwdstate_t{tile}",
    )(*args)


def _mlp_kernel(x_ref, mod_ref, g2_ref, w1_ref, w2_ref, fg_ref, out_ref, hm_s, acc_s, *,
                n_chunks, final_norm):
    k = pl.program_id(2)

    @pl.when(k == 0)
    def _():
        mod = mod_ref[0, 0]
        hm_s[...] = _modulate(x_ref[0], g2_ref[0], mod[4:5], mod[3:4]).astype(BF16)
        acc_s[...] = jnp.zeros_like(acc_s)

    hid = jnp.dot(hm_s[...], w1_ref[0], preferred_element_type=F32)
    hid = jnp.square(jnp.maximum(hid, 0.0))
    acc_s[...] += jnp.dot(hid.astype(BF16), w2_ref[0], preferred_element_type=F32)

    @pl.when(k == n_chunks - 1)
    def _():
        y = x_ref[0] + mod_ref[0, 0][5:6] * acc_s[...]
        if final_norm:
            ms = jnp.mean(y * y, axis=-1, keepdims=True)
            y = y * lax.rsqrt(ms + EPS) * fg_ref[...]
        out_ref[0] = y


def _mlp_call(x, mods, mod_row, P, l, fg, *, tile, final_norm):
    bsz, seq, d = x.shape
    n_chunks = D_FF // FF_CHUNK
    tok = pl.BlockSpec((1, tile, d), lambda b, j, k: (b, j, 0))
    return pl.pallas_call(
        functools.partial(_mlp_kernel, n_chunks=n_chunks, final_norm=final_norm),
        grid=(bsz, seq // tile, n_chunks),
        in_specs=[
            tok,
            pl.BlockSpec((1, 1, N_MOD, d), lambda b, j, k: (l, mod_row(b), 0, 0)),
            _layer_spec(P['g2'], l),
            pl.BlockSpec((1, d, FF_CHUNK), lambda b, j, k: (l, 0, k)),
            pl.BlockSpec((1, FF_CHUNK, d), lambda b, j, k: (l, k, 0)),
            _layer_spec(fg),
        ],
        out_specs=tok,
        out_shape=jax.ShapeDtypeStruct((bsz, seq, d), F32),
        scratch_shapes=[pltpu.VMEM((tile, d), BF16), pltpu.VMEM((tile, d), F32)],
        compiler_params=pltpu.CompilerParams(
            dimension_semantics=("arbitrary", "arbitrary", "arbitrary"),
            vmem_limit_bytes=VMEM_LIMIT),
        name=f"mlp_t{tile}",
    )(x, mods, P['g2'], P['w1'], P['w2'], fg)


def _bwd_mlp_kernel(uc_ref, hf_ref, gg_ref, gl_ref, gp_ref, x_ref, wg_ref, br_ref, bi_ref, lam_ref,
                    h0_ref, mod_mix_ref, mod_mlp_ref, wlo_ref, wo_ref, g2_ref, w1_ref, w2_ref,
                    fg_ref, out_ref, hl_s, x1_s, carry_s, *, tile, n_tiles, final_norm):
    s = pl.program_id(0)
    nq = N_GATE_GROUPS
    qcols = _group_cols()

    @pl.when(s == 0)
    def _():
        x1_s[...] = jnp.zeros_like(x1_s)

    @pl.when(s % n_tiles == 0)
    def _():
        carry_s[...] = h0_ref[0]

    cur = pl.ds(pl.multiple_of((s % 2) * tile, tile), tile)
    prev = pl.ds(pl.multiple_of(((s + 1) % 2) * tile, tile), tile)

    nla_rate, l2a_rate = _decay_rates(lam_ref[0, 0])
    br, bi = br_ref[0, 0], bi_ref[0, 0]
    state = {}

    half = tile // 2

    def coeffs(q, h):
        rws = slice(h * half, (h + 1) * half)
        state['ab', q, h] = _gate_coeffs(
            uc_ref[0, rws, qcols[q]], wg_ref[0, 0, q], br[:, qcols[q]], bi[:, qcols[q]],
            nla_rate[:, qcols[q]], l2a_rate[:, qcols[q]])

    def scan(q, h):
        def emit(r0, hrows):
            rws = slice(h * half + r0, h * half + r0 + SUBLANES)
            hl_s[rws, qcols[q]] = hrows + hf_ref[0, rws, qcols[q]]
        a, b = state.pop(('ab', q, h))
        carry_s[:, qcols[q]] = _scan_cols(a, b, carry_s[:, qcols[q]], emit, reverse=True)

    def lru(q):
        lru_in = (hl_s[:, qcols[q]] * gg_ref[0, :, qcols[q]].astype(F32)).astype(BF16)
        part = jnp.dot(lru_in, wlo_ref[0, qcols[q], :], preferred_element_type=F32)
        state['lru'] = part if q == 0 else state['lru'] + part

    def merge():
        merged = gl_ref[0].astype(F32) * state.pop('lru') + gp_ref[0].astype(F32)
        yo = jnp.dot(merged.astype(BF16), wo_ref[0], preferred_element_type=F32)
        x1_s[cur, :] = x_ref[0] + mod_mix_ref[0, 0][2:3] * yo

    mod = mod_mlp_ref[0, 0]

    def mlp_in():
        state['hm'] = _modulate(x1_s[prev, :], g2_ref[0], mod[4:5], mod[3:4]).astype(BF16)

    def mlp_up(k):
        cols = slice(k * MLP_FUSED_CHUNK, (k + 1) * MLP_FUSED_CHUNK)
        hid = jnp.dot(state['hm'], w1_ref[0, :, cols], preferred_element_type=F32)
        state['hid', k] = jnp.square(jnp.maximum(hid, 0.0)).astype(BF16)

    def mlp_down(k):
        cols = slice(k * MLP_FUSED_CHUNK, (k + 1) * MLP_FUSED_CHUNK)
        part = jnp.dot(state.pop(('hid', k)), w2_ref[0, cols, :], preferred_element_type=F32)
        state['acc'] = part if k == 0 else state['acc'] + part

    def mlp_out():
        y = x1_s[prev, :] + mod[5:6] * state.pop('acc')
        if final_norm:
            ms = jnp.mean(y * y, axis=-1, keepdims=True)
            y = y * lax.rsqrt(ms + EPS) * fg_ref[...]
        out_ref[0] = y

    n_chunks = D_FF // MLP_FUSED_CHUNK
    mxu_items = [lambda: mlp_up(0)]
    for k in range(n_chunks):
        if k + 1 < n_chunks:
            mxu_items.append(functools.partial(mlp_up, k + 1))
        mxu_items.append(functools.partial(mlp_down, k))
    vec_items = [lambda: coeffs(0, 1), lambda: coeffs(0, 0)]
    for q in range(nq):
        vec_items.append(functools.partial(scan, q, 1))
        if q + 1 < nq:
            vec_items.append(functools.partial(coeffs, q + 1, 1))

        def finish(q=q):
            scan(q, 0)
            lru(q)
        vec_items.append(finish)
        if q + 1 < nq:
            vec_items.append(functools.partial(coeffs, q + 1, 0))
    assert len(vec_items) == len(mxu_items)
    mlp_in()
    for vec, mxu in zip(vec_items, mxu_items):
        vec()
        mxu()
    mlp_out()
    merge()


def _bwd_mlp_call(uc, hf, gg, gl, gp, x, h0, mods, P, l, fg, *, tile, final_norm):
    bsz, seq, d = x.shape
    n_tiles = seq // tile
    n_steps = bsz * n_tiles

    def tile_of(s):
        s = jnp.clip(s, 0, n_steps - 1)
        return s // n_tiles, n_tiles - 1 - s % n_tiles

    def tok(shift):
        def index_map(s):
            b, j = tile_of(s - shift)
            return (b, j, 0)
        return pl.BlockSpec((1, tile, d), index_map)

    def mod_spec(shift):
        return pl.BlockSpec((1, 1, N_MOD, d), lambda s: (l, tile_of(s - shift)[0], 0, 0))

    args = [uc, hf, gg, gl, gp, x, P['wg'], P['br'], P['bi'], P['lam'], h0, mods, mods,
            P['wlo'], P['wo'], P['g2'], P['w1'], P['w2'], fg]
    in_specs = [tok(0)] * 6 + [
        _layer_spec(P['wg'], l, 1), _layer_spec(P['br'], l, 1), _layer_spec(P['bi'], l, 1),
        _layer_spec(P['lam'], l, 1),
        pl.BlockSpec((1, 1, D_LRU), lambda s: (tile_of(s)[0], 0, 0)),
        mod_spec(0), mod_spec(1),
        _layer_spec(P['wlo'], l), _layer_spec(P['wo'], l), _layer_spec(P['g2'], l),
        _layer_spec(P['w1'], l), _layer_spec(P['w2'], l), _layer_spec(fg)]
    return pl.pallas_call(
        functools.partial(_bwd_mlp_kernel, tile=tile, n_tiles=n_tiles, final_norm=final_norm),
        grid=(n_steps + 1,),
        in_specs=in_specs,
        out_specs=tok(1),
        out_shape=jax.ShapeDtypeStruct((bsz, seq, d), F32),
        scratch_shapes=[
            pltpu.VMEM((tile, D_LRU), F32),
            pltpu.VMEM((2 * tile, d), F32),
            pltpu.VMEM((1, D_LRU), F32),
        ],
        compiler_params=pltpu.CompilerParams(
            dimension_semantics=("arbitrary",), vmem_limit_bytes=VMEM_LIMIT),
        name=f"bwdmlp_t{tile}",
    )(*args)


def _pool_tables(tile, row_len):
    assert POOL_BLOCK % row_len == 0 and tile % POOL_BLOCK == 0
    t = np.arange(POOL_BLOCK)
    pos = t % row_len
    same_row = (t[:, None] // row_len) == (t[None, :] // row_len)
    mats, invs = [], []
    for w in POOL_WINDOWS:
        lo = np.clip(pos - w // 2, 0, row_len)
        hi = np.clip(pos + w - w // 2, 0, row_len)
        member = same_row & (pos[None, :] >= lo[:, None]) & (pos[None, :] < hi[:, None])
        mats.append(member.astype(np.float32))
        inv = np.repeat((1.0 / (hi - lo).astype(np.float32))[:, None], POOL_GROUP, axis=1)
        invs.append(np.tile(inv, (tile // POOL_BLOCK, 1)))
    return jnp.asarray(np.stack(mats), BF16), jnp.asarray(np.concatenate(invs, axis=1), F32)


def _gate_weights(w_r, w_i):
    hpg = GATE_GROUP // LRU_HEAD_DIM
    eye = jnp.eye(hpg, dtype=BF16)

    def blockdiag(w):
        w6 = w.astype(BF16).reshape(w.shape[0], 2, N_GATE_GROUPS, hpg, LRU_HEAD_DIM, LRU_HEAD_DIM)
        bd = jnp.einsum('ldqhie,hk->ldqhike', w6, eye)
        return bd.reshape(w.shape[0], 2, N_GATE_GROUPS, GATE_GROUP, GATE_GROUP)

    return jnp.concatenate([blockdiag(w_r), blockdiag(w_i)], axis=-1)


def kernel(x, c, ctx, c_ctx, w_ada, b_ada, norm1_g, norm2_g, w_in, conv_w, conv_b, lru_w_r, lru_b_r,
           lru_w_i, lru_b_i, lru_lambda, w_lru_out, pool_w, pool_scale, w_pool_out, w_o, mlp_w1,
           mlp_w2, final_g):
    bsz, seq, d = x.shape
    ctx_len = ctx.shape[1]
    depth = w_ada.shape[0]
    assert d == D_MODEL and seq % SEQ_TILE == 0
    assert bsz + 1 <= SUBLANES and ctx_len % HALO == 0

    cc = jnp.concatenate([c, c_ctx[None], jnp.zeros((SUBLANES - bsz - 1, d), F32)], axis=0)
    mods = _ada_call(cc, w_ada, b_ada).reshape(depth, SUBLANES, N_MOD, d)
    lat_row = lambda b: b
    ctx_row = lambda b: bsz

    pm_lat, pinv_lat = _pool_tables(SEQ_TILE, GRID_W)
    pm_ctx, pinv_ctx = _pool_tables(ctx_len, ctx_len)
    zeros_state = jnp.zeros((bsz, 1, D_LRU), F32)
    fg = final_g.reshape(1, d)
    rows = lambda v: v.reshape(v.shape[:-1] + (1, v.shape[-1]))
    P = dict(
        g1=rows(norm1_g), g2=rows(norm2_g), w_in=w_in.astype(BF16), cw=conv_w, cb=rows(conv_b),
        wg=_gate_weights(lru_w_r, lru_w_i), br=rows(lru_b_r), bi=rows(lru_b_i),
        lam=rows(lru_lambda), pw=pool_w.astype(BF16), ps=rows(pool_scale),
        wpo=w_pool_out.astype(BF16), wlo=w_lru_out.astype(BF16), wo=w_o.astype(BF16),
        w1=mlp_w1.astype(BF16), w2=mlp_w2.astype(BF16))

    for l in range(depth):
        last = l == depth - 1
        if last:
            uc_c, hfin_f = _fwd_call(ctx, mods, ctx_row, zeros_state, P, l, None, None,
                                     tile=ctx_len, full=False)
            hfin_b = _bwd_call(uc_c, zeros_state, P, l, tile=ctx_len, full=False)
        else:
            uc_c, hf_c, hfin_f, gg_c, gl_c, gp_c = _fwd_call(
                ctx, mods, ctx_row, zeros_state, P, l, pm_ctx, pinv_ctx, tile=ctx_len, full=True)
            hfin_b, ctx = _bwd_call(uc_c, zeros_state, P, l, ctx, hf_c, gg_c, gl_c, gp_c, mods,
                                    ctx_row, tile=ctx_len, full=True)
            ctx = _mlp_call(ctx, mods, ctx_row, P, l, fg, tile=ctx_len, final_norm=False)

        uc_l, hf_l, _, gg_l, gl_l, gp_l = _fwd_call(
            x, mods, lat_row, hfin_f, P, l, pm_lat, pinv_lat, tile=SEQ_TILE, full=True)
        x = _bwd_mlp_call(uc_l, hf_l, gg_l, gl_l, gp_l, x, hfin_b, mods, P, l, fg,
                          tile=SEQ_TILE, final_norm=last)
    return x
```

```python
import functools

import jax
import jax.numpy as jnp
import numpy as np
from jax import lax
from jax.experimental import pallas as pl
from jax.experimental.pallas import tpu as pltpu

D_MODEL = 1024
D_LRU = 1024
LRU_HEADS = 16
LRU_HEAD_DIM = D_LRU // LRU_HEADS
CONV_WIDTH = 4
CONV_LEFT = 2
LRU_C = 8.0
D_POOL = 512
POOL_WINDOWS = (2, 4, 8, 16)
POOL_GROUP = D_POOL // len(POOL_WINDOWS)
D_FF = 4 * D_MODEL
N_MOD = 6
GRID_W = 64
EPS = 1e-6
LOG2_E = 1.4426950408889634

SUBLANES = 8
HALO = 16
GATE_GROUP = 256
N_GATE_GROUPS = D_LRU // GATE_GROUP
ROW_BLOCK = 256
N_STEPS = ROW_BLOCK // SUBLANES
SEQ_TILE = 512
FF_CHUNK = 1024
MLP_FUSED_CHUNK = 512
ADA_CHUNK = 1536
VMEM_LIMIT = 56 * 1024 * 1024

F32 = jnp.float32
BF16 = jnp.bfloat16


def _layer_spec(arr, *lead):
    n_lead = len(lead)
    shape = (1,) * n_lead + arr.shape[n_lead:]
    idx = tuple(lead) + (0,) * (arr.ndim - n_lead)
    return pl.BlockSpec(shape, lambda *_: idx, pipeline_mode=pl.Buffered(1))


def _modulate(x, g, scale, shift):
    ms = jnp.mean(x * x, axis=-1, keepdims=True)
    return (x * lax.rsqrt(ms + EPS) * g) * (1.0 + scale) + shift


def _softplus(x):
    return jnp.maximum(x, 0.0) + jnp.log1p(jnp.exp(-jnp.abs(x)))


def _ada_kernel(c_ref, w_ref, b_ref, o_ref):
    c = c_ref[...]
    s = c * jax.nn.sigmoid(c)
    o_ref[0] = jnp.dot(s, w_ref[0], preferred_element_type=F32,
                       precision=lax.Precision.HIGHEST) + b_ref[0]


def _ada_call(cc, w_ada, b_ada):
    depth, d, n = w_ada.shape
    return pl.pallas_call(
        _ada_kernel,
        grid=(depth, n // ADA_CHUNK),
        in_specs=[
            pl.BlockSpec((SUBLANES, d), lambda l, k: (0, 0)),
            pl.BlockSpec((1, d, ADA_CHUNK), lambda l, k: (l, 0, k)),
            pl.BlockSpec((1, 1, ADA_CHUNK), lambda l, k: (l, 0, k)),
        ],
        out_specs=pl.BlockSpec((1, SUBLANES, ADA_CHUNK), lambda l, k: (l, 0, k)),
        out_shape=jax.ShapeDtypeStruct((depth, SUBLANES, n), F32),
        compiler_params=pltpu.CompilerParams(
            dimension_semantics=("arbitrary", "arbitrary"), vmem_limit_bytes=VMEM_LIMIT),
        name="adaln",
    )(cc, w_ada, b_ada.reshape(depth, 1, n))


def _decay_rates(lam):
    sp = _softplus(-lam)
    return LRU_C * sp, (-LRU_C * LOG2_E) * sp


def _gate_coeffs(uc, wg, br, bi, nla_rate, l2a_rate):
    z = jnp.dot(uc.astype(BF16), wg, preferred_element_type=F32)
    r = jax.nn.sigmoid(z[:, :GATE_GROUP] + br)
    i = jax.nn.sigmoid(z[:, GATE_GROUP:] + bi)
    a = jnp.exp2(r * l2a_rate)
    t = jnp.tanh(r * nla_rate) * (1.0 + a * a)
    mult = jnp.where(t > 0.0, t * lax.rsqrt(t), 0.0)
    return a, mult * (i * uc)


def _group_scan(ag, bg, c, reverse):
    sub = lax.broadcasted_iota(jnp.int32, ag.shape, 0)
    for k in (1, 2, 4):
        shift = (SUBLANES - k) if reverse else k
        m = (sub < SUBLANES - k) if reverse else (sub >= k)
        ra = pltpu.roll(ag, shift, 0)
        rb = pltpu.roll(bg, shift, 0)
        bg = jnp.where(m, ag * rb, 0.0) + bg
        ag = jnp.where(m, ag * ra, ag)
    return ag * c + bg


def _scan_block(a, b, c, emit, reverse):
    sub = lax.broadcasted_iota(jnp.int32, (SUBLANES, a.shape[1]), 0)
    local, decay = {}, {}
    h = acc = None
    for k in (range(N_STEPS - 1, -1, -1) if reverse else range(N_STEPS)):
        rws = slice(k * SUBLANES, (k + 1) * SUBLANES)
        h = b[rws] if h is None else a[rws] * h + b[rws]
        acc = a[rws] if acc is None else a[rws] * acc
        local[k], decay[k] = h, acc
    ends = _group_scan(acc, h, c, reverse)
    if reverse:
        start = jnp.where(sub == SUBLANES - 1, c, pltpu.roll(ends, SUBLANES - 1, 0))
        c = ends[0:1]
    else:
        start = jnp.where(sub == 0, c, pltpu.roll(ends, 1, 0))
        c = ends[SUBLANES - 1:SUBLANES]
    for k in range(N_STEPS):
        emit(k * SUBLANES, local[k] + decay[k] * start)
    return c


def _scan_blocks(a, b, c, emit, reverse):
    n_blocks = a.shape[0] // ROW_BLOCK
    for blk in (range(n_blocks - 1, -1, -1) if reverse else range(n_blocks)):
        base = blk * ROW_BLOCK
        rws = slice(base, base + ROW_BLOCK)
        c = _scan_block(a[rws], b[rws], c, lambda r0, h, base=base: emit(base + r0, h), reverse)
    return c


def _group_cols():
    return [slice(q * GATE_GROUP, (q + 1) * GATE_GROUP) for q in range(N_GATE_GROUPS)]


def _to_token_order(v, unperm_ref):
    blocks = [jnp.dot(unperm_ref[...], v[r0:r0 + ROW_BLOCK], preferred_element_type=F32)
              for r0 in range(0, v.shape[0], ROW_BLOCK)]
    return jnp.concatenate(blocks, axis=0).astype(BF16)


def _fwd_kernel(*refs, tile, n_tiles, full):
    if full:
        (x_ref, xp_ref, xn_ref, mod_ref, g1_ref, win_ref, cw_ref, cb_ref, wg_ref, br_ref, bi_ref,
         lam_ref, h0_ref, perm_ref, pm_ref, pinv_ref, pw_ref, ps_ref, wpo_ref,
         uc_ref, hf_ref, hfin_ref, gg_ref, gl_ref, gp_ref, carry_s) = refs
    else:
        (x_ref, xp_ref, xn_ref, mod_ref, g1_ref, win_ref, cw_ref, cb_ref, wg_ref, br_ref, bi_ref,
         lam_ref, h0_ref, perm_ref, uc_ref, hfin_ref, carry_s) = refs
    j = pl.program_id(1)
    nq = N_GATE_GROUPS
    qcols = _group_cols()

    @pl.when(j == 0)
    def _():
        carry_s[...] = h0_ref[0]

    mod = mod_ref[0, 0]
    x_ext = jnp.concatenate([xp_ref[0], x_ref[0], xn_ref[0]], axis=0)
    hb_ext = _modulate(x_ext, g1_ref[0], mod[1:2], mod[0:1]).astype(BF16)
    n_blocks = tile // ROW_BLOCK
    blocks = [jnp.dot(perm_ref[...], hb_ext[HALO + blk * ROW_BLOCK:HALO + (blk + 1) * ROW_BLOCK],
                      preferred_element_type=F32).astype(BF16) for blk in range(n_blocks)]
    lhs = jnp.concatenate(blocks + [hb_ext[0:HALO], hb_ext[HALO + tile:]], axis=0)
    hb = lhs[0:tile]
    keep_prev = jnp.where(j > 0, 1.0, 0.0)
    keep_next = jnp.where(j < n_tiles - 1, 1.0, 0.0)
    cw = cw_ref[0]
    cb = cb_ref[0]
    nla_rate, l2a_rate = _decay_rates(lam_ref[0, 0])
    br, bi = br_ref[0, 0], bi_ref[0, 0]
    o_g, o_p, o_t = D_LRU, 2 * D_LRU, 2 * D_LRU + D_POOL
    sub = lax.broadcasted_iota(jnp.int32, (SUBLANES, GATE_GROUP), 0)
    last = ROW_BLOCK - SUBLANES

    def w_cols(lo, width=GATE_GROUP):
        return win_ref[0, :, lo:lo + width]

    def conv(q):
        u = jnp.dot(lhs, w_cols(q * GATE_GROUP), preferred_element_type=F32)
        ublk = [u[blk * ROW_BLOCK:(blk + 1) * ROW_BLOCK] for blk in range(n_blocks)]
        before = u[tile + HALO - SUBLANES:tile + HALO] * keep_prev
        after = u[tile + HALO:tile + HALO + SUBLANES] * keep_next
        w = [cw[k:k + 1, qcols[q]] for k in range(CONV_WIDTH)]
        out = []
        for blk in range(n_blocks):
            cur = ublk[blk]
            if blk == 0:
                edge1, edge2 = pltpu.roll(before, 1, 0), pltpu.roll(before, 2, 0)
            else:
                edge1 = pltpu.roll(ublk[blk - 1][last:], 1, 0)
                edge2 = pltpu.roll(ublk[blk - 1][last - SUBLANES:last], 1, 0)
            nxt = after if blk == n_blocks - 1 else ublk[blk + 1][0:SUBLANES]
            head1 = jnp.where(sub == 0, edge1, pltpu.roll(cur[last:], 1, 0))
            head2 = jnp.where(sub == 0, edge2, pltpu.roll(cur[last - SUBLANES:last], 1, 0))
            tail = jnp.where(sub == SUBLANES - 1, pltpu.roll(nxt, SUBLANES - 1, 0),
                             pltpu.roll(cur[0:SUBLANES], SUBLANES - 1, 0))
            m1 = jnp.concatenate([head1, cur[:last]], axis=0)
            m2 = jnp.concatenate([head2, m1[:last]], axis=0)
            p1 = jnp.concatenate([cur[SUBLANES:], tail], axis=0)
            out.append(cb[:, qcols[q]] + w[0] * m2 + w[1] * m1 + w[2] * cur + w[3] * p1)
        uc = jnp.concatenate(out, axis=0)
        uc_ref[0, :, qcols[q]] = uc
        return uc

    def coeffs(q, uc):
        return _gate_coeffs(uc, wg_ref[0, 0, q], br[:, qcols[q]], bi[:, qcols[q]],
                            nla_rate[:, qcols[q]], l2a_rate[:, qcols[q]])

    def scan(q, ab):
        if full:
            def emit(r0, h):
                hf_ref[0, r0:r0 + SUBLANES, qcols[q]] = h
        else:
            def emit(r0, h):
                del r0, h
        c = _scan_blocks(ab[0], ab[1], carry_s[:, qcols[q]], emit, reverse=False)
        carry_s[:, qcols[q]] = c
        hfin_ref[0, :, qcols[q]] = c

    if not full:
        for q in range(nq):
            scan(q, coeffs(q, conv(q)))
        return

    def pool_group(gi, p, p_hi, p_lo):
        cols = slice(gi * POOL_GROUP, (gi + 1) * POOL_GROUP)
        wsum = []
        for r0 in range(0, tile, ROW_BLOCK):
            rws = slice(r0, r0 + ROW_BLOCK)
            rhs = jnp.concatenate([p_hi[rws, cols], p_lo[rws, cols]], axis=1)
            s2 = jnp.dot(pm_ref[gi], rhs, preferred_element_type=F32)
            wsum.append(s2[:, :POOL_GROUP] + s2[:, POOL_GROUP:])
        m = jnp.concatenate(wsum, axis=0) * pinv_ref[:, cols] - p[:, cols]
        return jnp.dot(m.astype(BF16), pw_ref[0, gi], preferred_element_type=F32)

    def gelu_cols(q):
        g = jnp.dot(hb, w_cols(o_g + q * GATE_GROUP), preferred_element_type=F32)
        gg_ref[0, :, qcols[q]] = jax.nn.gelu(g).astype(BF16)

    def lru_gate_cols(q):
        gt = jnp.dot(hb, w_cols(o_t + q * GATE_GROUP), preferred_element_type=F32)
        gl_ref[0, :, qcols[q]] = jax.nn.sigmoid(gt).astype(BF16)

    def pool_gate_cols(q, yb):
        gt = jnp.dot(hb, w_cols(o_t + D_MODEL + q * GATE_GROUP), preferred_element_type=F32)
        po = jnp.dot(yb, wpo_ref[0, :, qcols[q]], preferred_element_type=F32)
        gp_ref[0, :, qcols[q]] = (jax.nn.sigmoid(gt) * po).astype(BF16)

    ucs = [conv(0), conv(1)]
    p = jnp.dot(hb, w_cols(o_p, D_POOL), preferred_element_type=F32)
    p_hi = p.astype(BF16)
    p_lo = (p - p_hi.astype(F32)).astype(BF16)
    ucs.append(conv(2))
    ys = [pool_group(0, p, p_hi, p_lo), pool_group(1, p, p_hi, p_lo)]
    ucs.append(conv(3))
    ab = coeffs(0, ucs[0])
    ys += [pool_group(2, p, p_hi, p_lo), pool_group(3, p, p_hi, p_lo)]
    yb = (jnp.concatenate(ys, axis=-1) * ps_ref[0]).astype(BF16)
    for q in range(nq):
        gelu_cols(q)
        nxt = coeffs(q + 1, ucs[q + 1]) if q + 1 < nq else None
        lru_gate_cols(q)
        scan(q, ab)
        pool_gate_cols(q, yb)
        ab = nxt


def _fwd_call(x, mods, mod_row, h0, P, l, perm, pm, pinv, *, tile, full):
    bsz, seq, d = x.shape
    n_tiles = seq // tile
    hb = tile // HALO
    n_hb = seq // HALO
    tok = lambda b, j: (b, j, 0)
    tok_spec = pl.BlockSpec((1, tile, d), tok)
    state = pl.BlockSpec((1, 1, D_LRU), lambda b, j: (b, 0, 0))
    args = [x, x, x, mods, P['g1'], P['w_in'], P['cw'], P['cb'], P['wg'], P['br'], P['bi'],
            P['lam'], h0, perm]
    in_specs = [
        tok_spec,
        pl.BlockSpec((1, HALO, d), lambda b, j: (b, jnp.maximum(j * hb - 1, 0), 0)),
        pl.BlockSpec((1, HALO, d), lambda b, j: (b, jnp.minimum((j + 1) * hb, n_hb - 1), 0)),
        pl.BlockSpec((1, 1, N_MOD, d), lambda b, j: (l, mod_row(b), 0, 0)),
        _layer_spec(P['g1'], l), _layer_spec(P['w_in'], l), _layer_spec(P['cw'], l),
        _layer_spec(P['cb'], l), _layer_spec(P['wg'], l, 0), _layer_spec(P['br'], l, 0),
        _layer_spec(P['bi'], l, 0), _layer_spec(P['lam'], l, 0), state, _layer_spec(perm),
    ]
    f32_tok = jax.ShapeDtypeStruct((bsz, seq, D_LRU), F32)
    bf_tok = jax.ShapeDtypeStruct((bsz, seq, D_LRU), BF16)
    st = jax.ShapeDtypeStruct((bsz, 1, D_LRU), F32)
    if full:
        args += [pm, pinv, P['pw'], P['ps'], P['wpo']]
        in_specs += [_layer_spec(pm), _layer_spec(pinv), _layer_spec(P['pw'], l),
                     _layer_spec(P['ps'], l), _layer_spec(P['wpo'], l)]
        out_specs = [tok_spec, tok_spec, state, tok_spec, tok_spec, tok_spec]
        out_shape = [f32_tok, f32_tok, st, bf_tok, bf_tok, bf_tok]
    else:
        out_specs = [tok_spec, state]
        out_shape = [f32_tok, st]
    return pl.pallas_call(
        functools.partial(_fwd_kernel, tile=tile, n_tiles=n_tiles, full=full),
        grid=(bsz, n_tiles),
        in_specs=in_specs,
        out_specs=out_specs,
        out_shape=out_shape,
        scratch_shapes=[pltpu.VMEM((1, D_LRU), F32)],
        compiler_params=pltpu.CompilerParams(
            dimension_semantics=("arbitrary", "arbitrary"), vmem_limit_bytes=VMEM_LIMIT),
        name=f"fwd_t{tile}" if full else f"fwdstate_t{tile}",
    )(*args)


def _bwd_kernel(*refs, tile, full):
    if full:
        (uc_ref, wg_ref, br_ref, bi_ref, lam_ref, h0_ref, x_ref, hf_ref, gg_ref, gl_ref, gp_ref,
         mod_ref, wlo_ref, wo_ref, unperm_ref, hfin_ref, out_ref, hl_s, carry_s) = refs
    else:
        (uc_ref, wg_ref, br_ref, bi_ref, lam_ref, h0_ref, hfin_ref, carry_s) = refs
    j = pl.program_id(1)
    nq = N_GATE_GROUPS
    qcols = _group_cols()

    @pl.when(j == 0)
    def _():
        carry_s[...] = h0_ref[0]

    nla_rate, l2a_rate = _decay_rates(lam_ref[0, 0])
    br, bi = br_ref[0, 0], bi_ref[0, 0]

    def coeffs(q):
        return _gate_coeffs(uc_ref[0, :, qcols[q]], wg_ref[0, 0, q], br[:, qcols[q]],
                            bi[:, qcols[q]], nla_rate[:, qcols[q]], l2a_rate[:, qcols[q]])

    def scan(q, ab):
        if full:
            def emit(r0, h):
                rws = slice(r0, r0 + SUBLANES)
                hl_s[rws, qcols[q]] = h + hf_ref[0, rws, qcols[q]]
        else:
            def emit(r0, h):
                del r0, h
        c = _scan_blocks(ab[0], ab[1], carry_s[:, qcols[q]], emit, reverse=True)
        carry_s[:, qcols[q]] = c
        hfin_ref[0, :, qcols[q]] = c

    ab = coeffs(0)
    lru_out = None
    for q in range(nq):
        nxt = coeffs(q + 1) if q + 1 < nq else None
        scan(q, ab)
        ab = nxt
        if full:
            lru_in = (hl_s[:, qcols[q]] * gg_ref[0, :, qcols[q]].astype(F32)).astype(BF16)
            part = jnp.dot(lru_in, wlo_ref[0, qcols[q], :], preferred_element_type=F32)
            lru_out = part if lru_out is None else lru_out + part
    if not full:
        return

    merged = gl_ref[0].astype(F32) * lru_out + gp_ref[0].astype(F32)
    yo = jnp.dot(_to_token_order(merged.astype(BF16), unperm_ref), wo_ref[0],
                 preferred_element_type=F32)
    out_ref[0] = x_ref[0] + mod_ref[0, 0][2:3] * yo


def _bwd_call(uc, h0, P, l, x=None, hf=None, gg=None, gl=None, gp=None, mods=None, mod_row=None,
              unperm=None, *, tile, full):
    bsz, seq, d = uc.shape
    n_tiles = seq // tile
    tok = lambda b, j: (b, n_tiles - 1 - j, 0)
    tok_spec = pl.BlockSpec((1, tile, d), tok)
    state = pl.BlockSpec((1, 1, D_LRU), lambda b, j: (b, 0, 0))
    args = [uc, P['wg'], P['br'], P['bi'], P['lam'], h0]
    in_specs = [tok_spec, _layer_spec(P['wg'], l, 1), _layer_spec(P['br'], l, 1),
                _layer_spec(P['bi'], l, 1), _layer_spec(P['lam'], l, 1), state]
    st = jax.ShapeDtypeStruct((bsz, 1, D_LRU), F32)
    scratch = [pltpu.VMEM((1, D_LRU), F32)]
    if full:
        args += [x, hf, gg, gl, gp, mods, P['wlo'], P['wo'], unperm]
        in_specs += [tok_spec] * 5 + [
            pl.BlockSpec((1, 1, N_MOD, d), lambda b, j: (l, mod_row(b), 0, 0)),
            _layer_spec(P['wlo'], l), _layer_spec(P['wo'], l), _layer_spec(unperm)]
        out_specs = [state, tok_spec]
        out_shape = [st, jax.ShapeDtypeStruct((bsz, seq, d), F32)]
        scratch = [pltpu.VMEM((tile, D_LRU), F32)] + scratch
    else:
        out_specs = state
        out_shape = st
    return pl.pallas_call(
        functools.partial(_bwd_kernel, tile=tile, full=full),
        grid=(bsz, n_tiles),
        in_specs=in_specs,
        out_specs=out_specs,
        out_shape=out_shape,
        scratch_shapes=scratch,
        compiler_params=pltpu.CompilerParams(
            dimension_semantics=("arbitrary", "arbitrary"), vmem_limit_bytes=VMEM_LIMIT),
        name=f"bwd_t{tile}" if full else f"bwdstate_t{tile}",
    )(*args)


def _mlp_kernel(x_ref, mod_ref, g2_ref, w1_ref, w2_ref, fg_ref, out_ref, hm_s, acc_s, *,
                n_chunks, final_norm):
    k = pl.program_id(2)

    @pl.when(k == 0)
    def _():
        mod = mod_ref[0, 0]
        hm_s[...] = _modulate(x_ref[0], g2_ref[0], mod[4:5], mod[3:4]).astype(BF16)
        acc_s[...] = jnp.zeros_like(acc_s)

    hid = jnp.dot(hm_s[...], w1_ref[0], preferred_element_type=F32)
    hid = jnp.square(jnp.maximum(hid, 0.0))
    acc_s[...] += jnp.dot(hid.astype(BF16), w2_ref[0], preferred_element_type=F32)

    @pl.when(k == n_chunks - 1)
    def _():
        y = x_ref[0] + mod_ref[0, 0][5:6] * acc_s[...]
        if final_norm:
            ms = jnp.mean(y * y, axis=-1, keepdims=True)
            y = y * lax.rsqrt(ms + EPS) * fg_ref[...]
        out_ref[0] = y


def _mlp_call(x, mods, mod_row, P, l, fg, *, tile, final_norm):
    bsz, seq, d = x.shape
    n_chunks = D_FF // FF_CHUNK
    tok = pl.BlockSpec((1, tile, d), lambda b, j, k: (b, j, 0))
    return pl.pallas_call(
        functools.partial(_mlp_kernel, n_chunks=n_chunks, final_norm=final_norm),
        grid=(bsz, seq // tile, n_chunks),
        in_specs=[
            tok,
            pl.BlockSpec((1, 1, N_MOD, d), lambda b, j, k: (l, mod_row(b), 0, 0)),
            _layer_spec(P['g2'], l),
            pl.BlockSpec((1, d, FF_CHUNK), lambda b, j, k: (l, 0, k)),
            pl.BlockSpec((1, FF_CHUNK, d), lambda b, j, k: (l, k, 0)),
            _layer_spec(fg),
        ],
        out_specs=tok,
        out_shape=jax.ShapeDtypeStruct((bsz, seq, d), F32),
        scratch_shapes=[pltpu.VMEM((tile, d), BF16), pltpu.VMEM((tile, d), F32)],
        compiler_params=pltpu.CompilerParams(
            dimension_semantics=("arbitrary", "arbitrary", "arbitrary"),
            vmem_limit_bytes=VMEM_LIMIT),
        name=f"mlp_t{tile}",
    )(x, mods, P['g2'], P['w1'], P['w2'], fg)


def _bwd_mlp_kernel(uc_ref, hf_ref, gg_ref, gl_ref, gp_ref, x_ref, wg_ref, br_ref, bi_ref, lam_ref,
                    h0_ref, mod_mix_ref, mod_mlp_ref, wlo_ref, wo_ref, unperm_ref, g2_ref, w1_ref,
                    w2_ref, fg_ref, out_ref, hl_s, x1_s, carry_s, *, tile, n_tiles, final_norm):
    s = pl.program_id(0)
    nq = N_GATE_GROUPS
    qcols = _group_cols()

    @pl.when(s == 0)
    def _():
        x1_s[...] = jnp.zeros_like(x1_s)

    @pl.when(s % n_tiles == 0)
    def _():
        carry_s[...] = h0_ref[0]

    cur = pl.ds(pl.multiple_of((s % 2) * tile, tile), tile)
    prev = pl.ds(pl.multiple_of(((s + 1) % 2) * tile, tile), tile)

    nla_rate, l2a_rate = _decay_rates(lam_ref[0, 0])
    br, bi = br_ref[0, 0], bi_ref[0, 0]
    state = {}

    half = ROW_BLOCK
    assert tile == 2 * half

    def coeffs(q, h):
        rws = slice(h * half, (h + 1) * half)
        state['ab', q, h] = _gate_coeffs(
            uc_ref[0, rws, qcols[q]], wg_ref[0, 0, q], br[:, qcols[q]], bi[:, qcols[q]],
            nla_rate[:, qcols[q]], l2a_rate[:, qcols[q]])

    def scan(q, h):
        def emit(r0, hrows):
            rws = slice(h * half + r0, h * half + r0 + SUBLANES)
            hl_s[rws, qcols[q]] = hrows + hf_ref[0, rws, qcols[q]]
        a, b = state.pop(('ab', q, h))
        carry_s[:, qcols[q]] = _scan_block(a, b, carry_s[:, qcols[q]], emit, reverse=True)

    def lru(q):
        lru_in = (hl_s[:, qcols[q]] * gg_ref[0, :, qcols[q]].astype(F32)).astype(BF16)
        part = jnp.dot(lru_in, wlo_ref[0, qcols[q], :], preferred_element_type=F32)
        state['lru'] = part if q == 0 else state['lru'] + part

    def merge():
        merged = gl_ref[0].astype(F32) * state.pop('lru') + gp_ref[0].astype(F32)
        yo = jnp.dot(_to_token_order(merged.astype(BF16), unperm_ref), wo_ref[0],
                     preferred_element_type=F32)
        x1_s[cur, :] = x_ref[0] + mod_mix_ref[0, 0][2:3] * yo

    mod = mod_mlp_ref[0, 0]

    def mlp_in():
        state['hm'] = _modulate(x1_s[prev, :], g2_ref[0], mod[4:5], mod[3:4]).astype(BF16)

    def mlp_up(k):
        cols = slice(k * MLP_FUSED_CHUNK, (k + 1) * MLP_FUSED_CHUNK)
        hid = jnp.dot(state['hm'], w1_ref[0, :, cols], preferred_element_type=F32)
        state['hid', k] = jnp.square(jnp.maximum(hid, 0.0)).astype(BF16)

    def mlp_down(k):
        cols = slice(k * MLP_FUSED_CHUNK, (k + 1) * MLP_FUSED_CHUNK)
        part = jnp.dot(state.pop(('hid', k)), w2_ref[0, cols, :], preferred_element_type=F32)
        state['acc'] = part if k == 0 else state['acc'] + part

    def mlp_out():
        y = x1_s[prev, :] + mod[5:6] * state.pop('acc')
        if final_norm:
            ms = jnp.mean(y * y, axis=-1, keepdims=True)
            y = y * lax.rsqrt(ms + EPS) * fg_ref[...]
        out_ref[0] = y

    n_chunks = D_FF // MLP_FUSED_CHUNK
    mxu_items = [lambda: mlp_up(0)]
    for k in range(n_chunks):
        if k + 1 < n_chunks:
            mxu_items.append(functools.partial(mlp_up, k + 1))
        mxu_items.append(functools.partial(mlp_down, k))
    vec_items = [lambda: coeffs(0, 1), lambda: coeffs(0, 0)]
    for q in range(nq):
        vec_items.append(functools.partial(scan, q, 1))
        if q + 1 < nq:
            vec_items.append(functools.partial(coeffs, q + 1, 1))

        def finish(q=q):
            scan(q, 0)
            lru(q)
        vec_items.append(finish)
        if q + 1 < nq:
            vec_items.append(functools.partial(coeffs, q + 1, 0))
    assert len(vec_items) == len(mxu_items)
    mlp_in()
    for vec, mxu in zip(vec_items, mxu_items):
        vec()
        mxu()
    mlp_out()
    merge()


def _bwd_mlp_call(uc, hf, gg, gl, gp, x, h0, mods, P, l, unperm, fg, *, tile, final_norm):
    bsz, seq, d = x.shape
    n_tiles = seq // tile
    n_steps = bsz * n_tiles

    def tile_of(s):
        s = jnp.clip(s, 0, n_steps - 1)
        return s // n_tiles, n_tiles - 1 - s % n_tiles

    def tok(shift):
        def index_map(s):
            b, j = tile_of(s - shift)
            return (b, j, 0)
        return pl.BlockSpec((1, tile, d), index_map)

    def mod_spec(shift):
        return pl.BlockSpec((1, 1, N_MOD, d), lambda s: (l, tile_of(s - shift)[0], 0, 0))

    args = [uc, hf, gg, gl, gp, x, P['wg'], P['br'], P['bi'], P['lam'], h0, mods, mods,
            P['wlo'], P['wo'], unperm, P['g2'], P['w1'], P['w2'], fg]
    in_specs = [tok(0)] * 6 + [
        _layer_spec(P['wg'], l, 1), _layer_spec(P['br'], l, 1), _layer_spec(P['bi'], l, 1),
        _layer_spec(P['lam'], l, 1),
        pl.BlockSpec((1, 1, D_LRU), lambda s: (tile_of(s)[0], 0, 0)),
        mod_spec(0), mod_spec(1),
        _layer_spec(P['wlo'], l), _layer_spec(P['wo'], l), _layer_spec(unperm),
        _layer_spec(P['g2'], l), _layer_spec(P['w1'], l), _layer_spec(P['w2'], l), _layer_spec(fg)]
    return pl.pallas_call(
        functools.partial(_bwd_mlp_kernel, tile=tile, n_tiles=n_tiles, final_norm=final_norm),
        grid=(n_steps + 1,),
        in_specs=in_specs,
        out_specs=tok(1),
        out_shape=jax.ShapeDtypeStruct((bsz, seq, d), F32),
        scratch_shapes=[
            pltpu.VMEM((tile, D_LRU), F32),
            pltpu.VMEM((2 * tile, d), F32),
            pltpu.VMEM((1, D_LRU), F32),
        ],
        compiler_params=pltpu.CompilerParams(
            dimension_semantics=("arbitrary",), vmem_limit_bytes=VMEM_LIMIT),
        name=f"bwdmlp_t{tile}",
    )(*args)


def _interleave_tables():
    r = np.arange(ROW_BLOCK)
    token = N_STEPS * (r % SUBLANES) + r // SUBLANES
    perm = np.zeros((ROW_BLOCK, ROW_BLOCK), np.float32)
    perm[r, token] = 1.0
    return token, jnp.asarray(perm, BF16), jnp.asarray(perm.T, BF16)


def _pool_tables(tile, row_len, token):
    assert ROW_BLOCK % row_len == 0 and tile % ROW_BLOCK == 0
    pos = token % row_len
    same_row = (token[:, None] // row_len) == (token[None, :] // row_len)
    mats, invs = [], []
    for w in POOL_WINDOWS:
        lo = np.clip(pos - w // 2, 0, row_len)
        hi = np.clip(pos + w - w // 2, 0, row_len)
        member = same_row & (pos[None, :] >= lo[:, None]) & (pos[None, :] < hi[:, None])
        mats.append(member.astype(np.float32))
        inv = np.repeat((1.0 / (hi - lo).astype(np.float32))[:, None], POOL_GROUP, axis=1)
        invs.append(np.tile(inv, (tile // ROW_BLOCK, 1)))
    return jnp.asarray(np.stack(mats), BF16), jnp.asarray(np.concatenate(invs, axis=1), F32)


def _gate_weights(w_r, w_i):
    hpg = GATE_GROUP // LRU_HEAD_DIM
    eye = jnp.eye(hpg, dtype=BF16)

    def blockdiag(w):
        w6 = w.astype(BF16).reshape(w.shape[0], 2, N_GATE_GROUPS, hpg, LRU_HEAD_DIM, LRU_HEAD_DIM)
        bd = jnp.einsum('ldqhie,hk->ldqhike', w6, eye)
        return bd.reshape(w.shape[0], 2, N_GATE_GROUPS, GATE_GROUP, GATE_GROUP)

    return jnp.concatenate([blockdiag(w_r), blockdiag(w_i)], axis=-1)


def kernel(x, c, ctx, c_ctx, w_ada, b_ada, norm1_g, norm2_g, w_in, conv_w, conv_b, lru_w_r, lru_b_r,
           lru_w_i, lru_b_i, lru_lambda, w_lru_out, pool_w, pool_scale, w_pool_out, w_o, mlp_w1,
           mlp_w2, final_g):
    bsz, seq, d = x.shape
    ctx_len = ctx.shape[1]
    depth = w_ada.shape[0]
    assert d == D_MODEL and seq % SEQ_TILE == 0
    assert bsz + 1 <= SUBLANES and ctx_len % HALO == 0

    cc = jnp.concatenate([c, c_ctx[None], jnp.zeros((SUBLANES - bsz - 1, d), F32)], axis=0)
    mods = _ada_call(cc, w_ada, b_ada).reshape(depth, SUBLANES, N_MOD, d)
    lat_row = lambda b: b
    ctx_row = lambda b: bsz

    token, perm, unperm = _interleave_tables()
    pm_lat, pinv_lat = _pool_tables(SEQ_TILE, GRID_W, token)
    pm_ctx, pinv_ctx = _pool_tables(ctx_len, ctx_len, token)
    zeros_state = jnp.zeros((bsz, 1, D_LRU), F32)
    fg = final_g.reshape(1, d)
    rows = lambda v: v.reshape(v.shape[:-1] + (1, v.shape[-1]))
    P = dict(
        g1=rows(norm1_g), g2=rows(norm2_g), w_in=w_in.astype(BF16), cw=conv_w, cb=rows(conv_b),
        wg=_gate_weights(lru_w_r, lru_w_i), br=rows(lru_b_r), bi=rows(lru_b_i),
        lam=rows(lru_lambda), pw=pool_w.astype(BF16), ps=rows(pool_scale),
        wpo=w_pool_out.astype(BF16), wlo=w_lru_out.astype(BF16), wo=w_o.astype(BF16),
        w1=mlp_w1.astype(BF16), w2=mlp_w2.astype(BF16))

    for l in range(depth):
        last = l == depth - 1
        if last:
            uc_c, hfin_f = _fwd_call(ctx, mods, ctx_row, zeros_state, P, l, perm, None, None,
                                     tile=ctx_len, full=False)
            hfin_b = _bwd_call(uc_c, zeros_state, P, l, tile=ctx_len, full=False)
        else:
            uc_c, hf_c, hfin_f, gg_c, gl_c, gp_c = _fwd_call(
                ctx, mods, ctx_row, zeros_state, P, l, perm, pm_ctx, pinv_ctx, tile=ctx_len,
                full=True)
            hfin_b, ctx = _bwd_call(uc_c, zeros_state, P, l, ctx, hf_c, gg_c, gl_c, gp_c, mods,
                                    ctx_row, unperm, tile=ctx_len, full=True)
            ctx = _mlp_call(ctx, mods, ctx_row, P, l, fg, tile=ctx_len, final_norm=False)

        uc_l, hf_l, _, gg_l, gl_l, gp_l = _fwd_call(
            x, mods, lat_row, hfin_f, P, l, perm, pm_lat, pinv_lat, tile=SEQ_TILE, full=True)
        x = _bwd_mlp_call(uc_l, hf_l, gg_l, gl_l, gp_l, x, hfin_b, mods, P, l, unperm, fg,
                          tile=SEQ_TILE, final_norm=last)
    return x
```

```python
import functools

import jax
import jax.numpy as jnp
import numpy as np
from jax import lax
from jax.experimental import pallas as pl
from jax.experimental.pallas import tpu as pltpu

D_MODEL = 1024
D_LRU = 1024
LRU_HEADS = 16
LRU_HEAD_DIM = D_LRU // LRU_HEADS
CONV_WIDTH = 4
CONV_LEFT = 2
LRU_C = 8.0
D_POOL = 512
POOL_WINDOWS = (2, 4, 8, 16)
POOL_GROUP = D_POOL // len(POOL_WINDOWS)
D_FF = 4 * D_MODEL
N_MOD = 6
GRID_W = 64
EPS = 1e-6
LOG2_E = 1.4426950408889634

SUBLANES = 8
HALO = 16
GATE_GROUP = 256
N_GATE_GROUPS = D_LRU // GATE_GROUP
ROW_BLOCK = 256
N_STEPS = ROW_BLOCK // SUBLANES
SEQ_TILE = 512
FF_CHUNK = 1024
MLP_FUSED_CHUNK = 512
ADA_CHUNK = 1536
VMEM_LIMIT = 56 * 1024 * 1024

F32 = jnp.float32
BF16 = jnp.bfloat16


def _layer_spec(arr, *lead):
    n_lead = len(lead)
    shape = (1,) * n_lead + arr.shape[n_lead:]
    idx = tuple(lead) + (0,) * (arr.ndim - n_lead)
    return pl.BlockSpec(shape, lambda *_: idx, pipeline_mode=pl.Buffered(1))


def _modulate(x, g, scale, shift):
    ms = jnp.mean(x * x, axis=-1, keepdims=True)
    return (x * lax.rsqrt(ms + EPS) * g) * (1.0 + scale) + shift


def _softplus(x):
    return jnp.maximum(x, 0.0) + jnp.log1p(jnp.exp(-jnp.abs(x)))


GELU_C1 = 0.7978845608028654
GELU_C2 = 0.044715 * GELU_C1


def _gelu_tanh(x):
    hx = 0.5 * x
    return hx + hx * jnp.tanh(x * (GELU_C1 + GELU_C2 * (x * x)))


def _ada_kernel(c_ref, w_ref, b_ref, o_ref):
    c = c_ref[...]
    s = c * jax.nn.sigmoid(c)
    o_ref[0] = jnp.dot(s, w_ref[0], preferred_element_type=F32,
                       precision=lax.Precision.HIGHEST) + b_ref[0]


def _ada_call(cc, w_ada, b_ada):
    depth, d, n = w_ada.shape
    return pl.pallas_call(
        _ada_kernel,
        grid=(depth, n // ADA_CHUNK),
        in_specs=[
            pl.BlockSpec((SUBLANES, d), lambda l, k: (0, 0)),
            pl.BlockSpec((1, d, ADA_CHUNK), lambda l, k: (l, 0, k)),
            pl.BlockSpec((1, 1, ADA_CHUNK), lambda l, k: (l, 0, k)),
        ],
        out_specs=pl.BlockSpec((1, SUBLANES, ADA_CHUNK), lambda l, k: (l, 0, k)),
        out_shape=jax.ShapeDtypeStruct((depth, SUBLANES, n), F32),
        compiler_params=pltpu.CompilerParams(
            dimension_semantics=("arbitrary", "arbitrary"), vmem_limit_bytes=VMEM_LIMIT),
        name="adaln",
    )(cc, w_ada, b_ada.reshape(depth, 1, n))


def _decay_rates(lam):
    sp = _softplus(-lam)
    return LRU_C * sp, (-LRU_C * LOG2_E) * sp


def _gate_coeffs(uc, wg, br, bi, nla_rate, l2a_rate):
    z = jnp.dot(uc.astype(BF16), wg, preferred_element_type=F32)
    r = jax.nn.sigmoid(z[:, :GATE_GROUP] + br)
    i = jax.nn.sigmoid(z[:, GATE_GROUP:] + bi)
    a = jnp.exp2(r * l2a_rate)
    t = jnp.tanh(r * nla_rate) * (1.0 + a * a)
    mult = jnp.where(t > 0.0, t * lax.rsqrt(t), 0.0)
    return a, mult * (i * uc)


def _group_scan(ag, bg, c, reverse):
    sub = lax.broadcasted_iota(jnp.int32, ag.shape, 0)
    for k in (1, 2, 4):
        shift = (SUBLANES - k) if reverse else k
        m = (sub < SUBLANES - k) if reverse else (sub >= k)
        ra = pltpu.roll(ag, shift, 0)
        rb = pltpu.roll(bg, shift, 0)
        bg = jnp.where(m, ag * rb, 0.0) + bg
        ag = jnp.where(m, ag * ra, ag)
    return ag * c + bg


def _scan_block(a, b, c, emit, reverse):
    sub = lax.broadcasted_iota(jnp.int32, (SUBLANES, a.shape[1]), 0)
    local, decay = {}, {}
    h = acc = None
    for k in (range(N_STEPS - 1, -1, -1) if reverse else range(N_STEPS)):
        rws = slice(k * SUBLANES, (k + 1) * SUBLANES)
        h = b[rws] if h is None else a[rws] * h + b[rws]
        acc = a[rws] if acc is None else a[rws] * acc
        local[k], decay[k] = h, acc
    ends = _group_scan(acc, h, c, reverse)
    if reverse:
        start = jnp.where(sub == SUBLANES - 1, c, pltpu.roll(ends, SUBLANES - 1, 0))
        c = ends[0:1]
    else:
        start = jnp.where(sub == 0, c, pltpu.roll(ends, 1, 0))
        c = ends[SUBLANES - 1:SUBLANES]
    for k in range(N_STEPS):
        emit(k * SUBLANES, local[k] + decay[k] * start)
    return c


def _scan_blocks(a, b, c, emit, reverse):
    n_blocks = a.shape[0] // ROW_BLOCK
    for blk in (range(n_blocks - 1, -1, -1) if reverse else range(n_blocks)):
        base = blk * ROW_BLOCK
        rws = slice(base, base + ROW_BLOCK)
        c = _scan_block(a[rws], b[rws], c, lambda r0, h, base=base: emit(base + r0, h), reverse)
    return c


def _group_cols():
    return [slice(q * GATE_GROUP, (q + 1) * GATE_GROUP) for q in range(N_GATE_GROUPS)]


def _to_token_order(v, unperm_ref):
    blocks = [jnp.dot(unperm_ref[...], v[r0:r0 + ROW_BLOCK], preferred_element_type=F32)
              for r0 in range(0, v.shape[0], ROW_BLOCK)]
    return jnp.concatenate(blocks, axis=0).astype(BF16)


def _fwd_kernel(*refs, tile, n_tiles, full):
    if full:
        (x_ref, xp_ref, xn_ref, mod_ref, g1_ref, win_ref, cw_ref, cb_ref, wg_ref, br_ref, bi_ref,
         lam_ref, h0_ref, perm_ref, pm_ref, pinv_ref, pw_ref, ps_ref, wpo_ref,
         uc_ref, hf_ref, hfin_ref, gg_ref, gl_ref, gp_ref, lhs_s, carry_s) = refs
    else:
        (x_ref, xp_ref, xn_ref, mod_ref, g1_ref, win_ref, cw_ref, cb_ref, wg_ref, br_ref, bi_ref,
         lam_ref, h0_ref, perm_ref, uc_ref, hfin_ref, lhs_s, carry_s) = refs
    j = pl.program_id(1)
    nq = N_GATE_GROUPS
    qcols = _group_cols()

    @pl.when(j == 0)
    def _():
        carry_s[...] = h0_ref[0]

    mod = mod_ref[0, 0]
    x_ext = jnp.concatenate([xp_ref[0], x_ref[0], xn_ref[0]], axis=0)
    hb_ext = _modulate(x_ext, g1_ref[0], mod[1:2], mod[0:1]).astype(BF16)
    n_blocks = tile // ROW_BLOCK
    blocks = [jnp.dot(perm_ref[...], hb_ext[HALO + blk * ROW_BLOCK:HALO + (blk + 1) * ROW_BLOCK],
                      preferred_element_type=F32).astype(BF16) for blk in range(n_blocks)]
    lhs_s[...] = jnp.concatenate(blocks + [hb_ext[0:HALO], hb_ext[HALO + tile:]], axis=0)
    keep_prev = jnp.where(j > 0, 1.0, 0.0)
    keep_next = jnp.where(j < n_tiles - 1, 1.0, 0.0)
    cw = cw_ref[0]
    cb = cb_ref[0]
    nla_rate, l2a_rate = _decay_rates(lam_ref[0, 0])
    br, bi = br_ref[0, 0], bi_ref[0, 0]
    o_g, o_p, o_t = D_LRU, 2 * D_LRU, 2 * D_LRU + D_POOL
    sub = lax.broadcasted_iota(jnp.int32, (SUBLANES, GATE_GROUP), 0)
    last = ROW_BLOCK - SUBLANES

    def w_cols(lo, width=GATE_GROUP):
        return win_ref[0, :, lo:lo + width]

    def conv(q):
        u = jnp.dot(lhs_s[...], w_cols(q * GATE_GROUP), preferred_element_type=F32)
        ublk = [u[blk * ROW_BLOCK:(blk + 1) * ROW_BLOCK] for blk in range(n_blocks)]
        before = u[tile + HALO - SUBLANES:tile + HALO] * keep_prev
        after = u[tile + HALO:tile + HALO + SUBLANES] * keep_next
        w = [cw[k:k + 1, qcols[q]] for k in range(CONV_WIDTH)]
        out = []
        for blk in range(n_blocks):
            cur = ublk[blk]
            if blk == 0:
                edge1, edge2 = pltpu.roll(before, 1, 0), pltpu.roll(before, 2, 0)
            else:
                edge1 = pltpu.roll(ublk[blk - 1][last:], 1, 0)
                edge2 = pltpu.roll(ublk[blk - 1][last - SUBLANES:last], 1, 0)
            nxt = after if blk == n_blocks - 1 else ublk[blk + 1][0:SUBLANES]
            head1 = jnp.where(sub == 0, edge1, pltpu.roll(cur[last:], 1, 0))
            head2 = jnp.where(sub == 0, edge2, pltpu.roll(cur[last - SUBLANES:last], 1, 0))
            tail = jnp.where(sub == SUBLANES - 1, pltpu.roll(nxt, SUBLANES - 1, 0),
                             pltpu.roll(cur[0:SUBLANES], SUBLANES - 1, 0))
            m1 = jnp.concatenate([head1, cur[:last]], axis=0)
            m2 = jnp.concatenate([head2, m1[:last]], axis=0)
            p1 = jnp.concatenate([cur[SUBLANES:], tail], axis=0)
            out.append(cb[:, qcols[q]] + w[0] * m2 + w[1] * m1 + w[2] * cur + w[3] * p1)
        uc = jnp.concatenate(out, axis=0)
        uc_ref[0, :, qcols[q]] = uc
        return uc

    def coeffs(q, uc, blk):
        rws = slice(blk * ROW_BLOCK, (blk + 1) * ROW_BLOCK)
        return _gate_coeffs(uc[rws], wg_ref[0, 0, q], br[:, qcols[q]], bi[:, qcols[q]],
                            nla_rate[:, qcols[q]], l2a_rate[:, qcols[q]])

    def scan(q, ab):
        c = carry_s[:, qcols[q]]
        for blk, (a, b) in enumerate(ab):
            if full:
                def emit(r0, h, base=blk * ROW_BLOCK):
                    hf_ref[0, base + r0:base + r0 + SUBLANES, qcols[q]] = h
            else:
                def emit(r0, h):
                    del r0, h
            c = _scan_block(a, b, c, emit, reverse=False)
        carry_s[:, qcols[q]] = c
        hfin_ref[0, :, qcols[q]] = c

    if not full:
        for q in range(nq):
            uc = conv(q)
            scan(q, [coeffs(q, uc, blk) for blk in range(n_blocks)])
        return

    def pool_mean(gi, p, p_hi, p_lo):
        cols = slice(gi * POOL_GROUP, (gi + 1) * POOL_GROUP)
        wsum = []
        for r0 in range(0, tile, ROW_BLOCK):
            rws = slice(r0, r0 + ROW_BLOCK)
            rhs = jnp.concatenate([p_hi[rws, cols], p_lo[rws, cols]], axis=1)
            s2 = jnp.dot(pm_ref[gi], rhs, preferred_element_type=F32)
            wsum.append(s2[:, :POOL_GROUP] + s2[:, POOL_GROUP:])
        return (jnp.concatenate(wsum, axis=0) * pinv_ref[:, cols] - p[:, cols]).astype(BF16)

    def pool_mix(gi, m):
        return jnp.dot(m, pw_ref[0, gi], preferred_element_type=F32)

    def gelu_cols(q):
        g = jnp.dot(lhs_s[0:tile], w_cols(o_g + q * GATE_GROUP), preferred_element_type=F32)
        gg_ref[0, :, qcols[q]] = _gelu_tanh(g).astype(BF16)

    def lru_gate_cols(q):
        gt = jnp.dot(lhs_s[0:tile], w_cols(o_t + q * GATE_GROUP), preferred_element_type=F32)
        gl_ref[0, :, qcols[q]] = jax.nn.sigmoid(gt).astype(BF16)

    def pool_gate_cols(q, yb):
        gt = jnp.dot(lhs_s[0:tile], w_cols(o_t + D_MODEL + q * GATE_GROUP),
                     preferred_element_type=F32)
        po = jnp.dot(yb, wpo_ref[0, :, qcols[q]], preferred_element_type=F32)
        gp_ref[0, :, qcols[q]] = (jax.nn.sigmoid(gt) * po).astype(BF16)

    ucs = [conv(0), conv(1)]
    p = jnp.dot(lhs_s[0:tile], w_cols(o_p, D_POOL), preferred_element_type=F32)
    p_hi = p.astype(BF16)
    p_lo = (p - p_hi.astype(F32)).astype(BF16)
    ucs.append(conv(2))
    ys = [pool_mix(gi, pool_mean(gi, p, p_hi, p_lo)) for gi in (0, 1)]
    ucs.append(conv(3))
    ab = [coeffs(0, ucs[0], blk) for blk in range(n_blocks)]
    ys += [pool_mix(gi, pool_mean(gi, p, p_hi, p_lo)) for gi in (2, 3)]
    yb = (jnp.concatenate(ys, axis=-1) * ps_ref[0]).astype(BF16)
    for q in range(nq):
        nxt = []
        gelu_cols(q)
        if q + 1 < nq:
            nxt.append(coeffs(q + 1, ucs[q + 1], 0))
        lru_gate_cols(q)
        if q + 1 < nq:
            nxt += [coeffs(q + 1, ucs[q + 1], blk) for blk in range(1, n_blocks)]
        scan(q, ab)
        pool_gate_cols(q, yb)
        ab = nxt


def _fwd_call(x, mods, mod_row, h0, P, l, perm, pm, pinv, *, tile, full):
    bsz, seq, d = x.shape
    n_tiles = seq // tile
    hb = tile // HALO
    n_hb = seq // HALO
    tok = lambda b, j: (b, j, 0)
    tok_spec = pl.BlockSpec((1, tile, d), tok)
    state = pl.BlockSpec((1, 1, D_LRU), lambda b, j: (b, 0, 0))
    args = [x, x, x, mods, P['g1'], P['w_in'], P['cw'], P['cb'], P['wg'], P['br'], P['bi'],
            P['lam'], h0, perm]
    in_specs = [
        tok_spec,
        pl.BlockSpec((1, HALO, d), lambda b, j: (b, jnp.maximum(j * hb - 1, 0), 0)),
        pl.BlockSpec((1, HALO, d), lambda b, j: (b, jnp.minimum((j + 1) * hb, n_hb - 1), 0)),
        pl.BlockSpec((1, 1, N_MOD, d), lambda b, j: (l, mod_row(b), 0, 0)),
        _layer_spec(P['g1'], l), _layer_spec(P['w_in'], l), _layer_spec(P['cw'], l),
        _layer_spec(P['cb'], l), _layer_spec(P['wg'], l, 0), _layer_spec(P['br'], l, 0),
        _layer_spec(P['bi'], l, 0), _layer_spec(P['lam'], l, 0), state, _layer_spec(perm),
    ]
    f32_tok = jax.ShapeDtypeStruct((bsz, seq, D_LRU), F32)
    bf_tok = jax.ShapeDtypeStruct((bsz, seq, D_LRU), BF16)
    st = jax.ShapeDtypeStruct((bsz, 1, D_LRU), F32)
    if full:
        args += [pm, pinv, P['pw'], P['ps'], P['wpo']]
        in_specs += [_layer_spec(pm), _layer_spec(pinv), _layer_spec(P['pw'], l),
                     _layer_spec(P['ps'], l), _layer_spec(P['wpo'], l)]
        out_specs = [tok_spec, tok_spec, state, tok_spec, tok_spec, tok_spec]
        out_shape = [f32_tok, f32_tok, st, bf_tok, bf_tok, bf_tok]
    else:
        out_specs = [tok_spec, state]
        out_shape = [f32_tok, st]
    return pl.pallas_call(
        functools.partial(_fwd_kernel, tile=tile, n_tiles=n_tiles, full=full),
        grid=(bsz, n_tiles),
        in_specs=in_specs,
        out_specs=out_specs,
        out_shape=out_shape,
        scratch_shapes=[pltpu.VMEM((tile + 2 * HALO, d), BF16), pltpu.VMEM((1, D_LRU), F32)],
        compiler_params=pltpu.CompilerParams(
            dimension_semantics=("arbitrary", "arbitrary"), vmem_limit_bytes=VMEM_LIMIT),
        name=f"fwd_t{tile}" if full else f"fwdstate_t{tile}",
    )(*args)


def _bwd_kernel(*refs, tile, full):
    if full:
        (uc_ref, wg_ref, br_ref, bi_ref, lam_ref, h0_ref, x_ref, hf_ref, gg_ref, gl_ref, gp_ref,
         mod_ref, wlo_ref, wo_ref, unperm_ref, hfin_ref, out_ref, hl_s, carry_s) = refs
    else:
        (uc_ref, wg_ref, br_ref, bi_ref, lam_ref, h0_ref, hfin_ref, carry_s) = refs
    j = pl.program_id(1)
    nq = N_GATE_GROUPS
    qcols = _group_cols()

    @pl.when(j == 0)
    def _():
        carry_s[...] = h0_ref[0]

    nla_rate, l2a_rate = _decay_rates(lam_ref[0, 0])
    br, bi = br_ref[0, 0], bi_ref[0, 0]

    def coeffs(q):
        return _gate_coeffs(uc_ref[0, :, qcols[q]], wg_ref[0, 0, q], br[:, qcols[q]],
                            bi[:, qcols[q]], nla_rate[:, qcols[q]], l2a_rate[:, qcols[q]])

    def scan(q, ab):
        if full:
            def emit(r0, h):
                rws = slice(r0, r0 + SUBLANES)
                hl_s[rws, qcols[q]] = h + hf_ref[0, rws, qcols[q]]
        else:
            def emit(r0, h):
                del r0, h
        c = _scan_blocks(ab[0], ab[1], carry_s[:, qcols[q]], emit, reverse=True)
        carry_s[:, qcols[q]] = c
        hfin_ref[0, :, qcols[q]] = c

    ab = coeffs(0)
    lru_out = None
    for q in range(nq):
        nxt = coeffs(q + 1) if q + 1 < nq else None
        scan(q, ab)
        ab = nxt
        if full:
            lru_in = (hl_s[:, qcols[q]] * gg_ref[0, :, qcols[q]].astype(F32)).astype(BF16)
            part = jnp.dot(lru_in, wlo_ref[0, qcols[q], :], preferred_element_type=F32)
            lru_out = part if lru_out is None else lru_out + part
    if not full:
        return

    merged = gl_ref[0].astype(F32) * lru_out + gp_ref[0].astype(F32)
    yo = jnp.dot(_to_token_order(merged.astype(BF16), unperm_ref), wo_ref[0],
                 preferred_element_type=F32)
    out_ref[0] = x_ref[0] + mod_ref[0, 0][2:3] * yo


def _bwd_call(uc, h0, P, l, x=None, hf=None, gg=None, gl=None, gp=None, mods=None, mod_row=None,
              unperm=None, *, tile, full):
    bsz, seq, d = uc.shape
    n_tiles = seq // tile
    tok = lambda b, j: (b, n_tiles - 1 - j, 0)
    tok_spec = pl.BlockSpec((1, tile, d), tok)
    state = pl.BlockSpec((1, 1, D_LRU), lambda b, j: (b, 0, 0))
    args = [uc, P['wg'], P['br'], P['bi'], P['lam'], h0]
    in_specs = [tok_spec, _layer_spec(P['wg'], l, 1), _layer_spec(P['br'], l, 1),
                _layer_spec(P['bi'], l, 1), _layer_spec(P['lam'], l, 1), state]
    st = jax.ShapeDtypeStruct((bsz, 1, D_LRU), F32)
    scratch = [pltpu.VMEM((1, D_LRU), F32)]
    if full:
        args += [x, hf, gg, gl, gp, mods, P['wlo'], P['wo'], unperm]
        in_specs += [tok_spec] * 5 + [
            pl.BlockSpec((1, 1, N_MOD, d), lambda b, j: (l, mod_row(b), 0, 0)),
            _layer_spec(P['wlo'], l), _layer_spec(P['wo'], l), _layer_spec(unperm)]
        out_specs = [state, tok_spec]
        out_shape = [st, jax.ShapeDtypeStruct((bsz, seq, d), F32)]
        scratch = [pltpu.VMEM((tile, D_LRU), F32)] + scratch
    else:
        out_specs = state
        out_shape = st
    return pl.pallas_call(
        functools.partial(_bwd_kernel, tile=tile, full=full),
        grid=(bsz, n_tiles),
        in_specs=in_specs,
        out_specs=out_specs,
        out_shape=out_shape,
        scratch_shapes=scratch,
        compiler_params=pltpu.CompilerParams(
            dimension_semantics=("arbitrary", "arbitrary"), vmem_limit_bytes=VMEM_LIMIT),
        name=f"bwd_t{tile}" if full else f"bwdstate_t{tile}",
    )(*args)


def _mlp_kernel(x_ref, mod_ref, g2_ref, w1_ref, w2_ref, fg_ref, out_ref, hm_s, acc_s, *,
                n_chunks, final_norm):
    k = pl.program_id(2)

    @pl.when(k == 0)
    def _():
        mod = mod_ref[0, 0]
        hm_s[...] = _modulate(x_ref[0], g2_ref[0], mod[4:5], mod[3:4]).astype(BF16)
        acc_s[...] = jnp.zeros_like(acc_s)

    hid = jnp.dot(hm_s[...], w1_ref[0], preferred_element_type=F32)
    hid = jnp.square(jnp.maximum(hid, 0.0))
    acc_s[...] += jnp.dot(hid.astype(BF16), w2_ref[0], preferred_element_type=F32)

    @pl.when(k == n_chunks - 1)
    def _():
        y = x_ref[0] + mod_ref[0, 0][5:6] * acc_s[...]
        if final_norm:
            ms = jnp.mean(y * y, axis=-1, keepdims=True)
            y = y * lax.rsqrt(ms + EPS) * fg_ref[...]
        out_ref[0] = y


def _mlp_call(x, mods, mod_row, P, l, fg, *, tile, final_norm):
    bsz, seq, d = x.shape
    n_chunks = D_FF // FF_CHUNK
    tok = pl.BlockSpec((1, tile, d), lambda b, j, k: (b, j, 0))
    return pl.pallas_call(
        functools.partial(_mlp_kernel, n_chunks=n_chunks, final_norm=final_norm),
        grid=(bsz, seq // tile, n_chunks),
        in_specs=[
            tok,
            pl.BlockSpec((1, 1, N_MOD, d), lambda b, j, k: (l, mod_row(b), 0, 0)),
            _layer_spec(P['g2'], l),
            pl.BlockSpec((1, d, FF_CHUNK), lambda b, j, k: (l, 0, k)),
            pl.BlockSpec((1, FF_CHUNK, d), lambda b, j, k: (l, k, 0)),
            _layer_spec(fg),
        ],
        out_specs=tok,
        out_shape=jax.ShapeDtypeStruct((bsz, seq, d), F32),
        scratch_shapes=[pltpu.VMEM((tile, d), BF16), pltpu.VMEM((tile, d), F32)],
        compiler_params=pltpu.CompilerParams(
            dimension_semantics=("arbitrary", "arbitrary", "arbitrary"),
            vmem_limit_bytes=VMEM_LIMIT),
        name=f"mlp_t{tile}",
    )(x, mods, P['g2'], P['w1'], P['w2'], fg)


def _bwd_mlp_kernel(uc_ref, hf_ref, gg_ref, gl_ref, gp_ref, x_ref, wg_ref, br_ref, bi_ref, lam_ref,
                    h0_ref, mod_mix_ref, mod_mlp_ref, wlo_ref, wo_ref, unperm_ref, g2_ref, w1_ref,
                    w2_ref, fg_ref, out_ref, hl_s, x1_s, hm_s, carry_s, *, tile, n_tiles,
                    final_norm):
    s = pl.program_id(0)
    nq = N_GATE_GROUPS
    qcols = _group_cols()

    @pl.when(s == 0)
    def _():
        x1_s[...] = jnp.zeros_like(x1_s)

    @pl.when(s % n_tiles == 0)
    def _():
        carry_s[...] = h0_ref[0]

    nla_rate, l2a_rate = _decay_rates(lam_ref[0, 0])
    br, bi = br_ref[0, 0], bi_ref[0, 0]
    state = {}

    half = ROW_BLOCK
    assert tile == 2 * half

    def coeffs(q, h):
        rws = slice(h * half, (h + 1) * half)
        state['ab', q, h] = _gate_coeffs(
            uc_ref[0, rws, qcols[q]], wg_ref[0, 0, q], br[:, qcols[q]], bi[:, qcols[q]],
            nla_rate[:, qcols[q]], l2a_rate[:, qcols[q]])

    def scan(q, h):
        def emit(r0, hrows):
            rws = slice(h * half + r0, h * half + r0 + SUBLANES)
            hl_s[rws, qcols[q]] = hrows + hf_ref[0, rws, qcols[q]]
        a, b = state.pop(('ab', q, h))
        carry_s[:, qcols[q]] = _scan_block(a, b, carry_s[:, qcols[q]], emit, reverse=True)

    def lru(q):
        lru_in = (hl_s[:, qcols[q]] * gg_ref[0, :, qcols[q]].astype(F32)).astype(BF16)
        part = jnp.dot(lru_in, wlo_ref[0, qcols[q], :], preferred_element_type=F32)
        state['lru'] = part if q == 0 else state['lru'] + part

    def merge():
        merged = gl_ref[0].astype(F32) * state.pop('lru') + gp_ref[0].astype(F32)
        state['merged'] = _to_token_order(merged.astype(BF16), unperm_ref)

    def project():
        yo = jnp.dot(state.pop('merged'), wo_ref[0], preferred_element_type=F32)
        state['x1'] = x_ref[0] + mod_mix_ref[0, 0][2:3] * yo

    mod = mod_mlp_ref[0, 0]

    def mlp_in():
        hm_s[...] = _modulate(x1_s[...], g2_ref[0], mod[4:5], mod[3:4]).astype(BF16)

    def mlp_up(k):
        cols = slice(k * MLP_FUSED_CHUNK, (k + 1) * MLP_FUSED_CHUNK)
        hid = jnp.dot(hm_s[...], w1_ref[0, :, cols], preferred_element_type=F32)
        state['hid', k] = jnp.square(jnp.maximum(hid, 0.0)).astype(BF16)

    def mlp_down(k):
        cols = slice(k * MLP_FUSED_CHUNK, (k + 1) * MLP_FUSED_CHUNK)
        part = jnp.dot(state.pop(('hid', k)), w2_ref[0, cols, :], preferred_element_type=F32)
        state['acc'] = part if k == 0 else state['acc'] + part

    def mlp_out():
        y = x1_s[...] + mod[5:6] * state.pop('acc')
        if final_norm:
            ms = jnp.mean(y * y, axis=-1, keepdims=True)
            y = y * lax.rsqrt(ms + EPS) * fg_ref[...]
        out_ref[0] = y

    n_chunks = D_FF // MLP_FUSED_CHUNK
    mxu_items = [lambda: mlp_up(0)]
    for k in range(n_chunks):
        if k + 1 < n_chunks:
            mxu_items.append(functools.partial(mlp_up, k + 1))
        mxu_items.append(functools.partial(mlp_down, k))
    vec_items = [lambda: coeffs(0, 1), lambda: coeffs(0, 0)]
    for q in range(nq):
        vec_items.append(functools.partial(scan, q, 1))
        if q + 1 < nq:
            vec_items.append(functools.partial(coeffs, q + 1, 1))

        def finish(q=q):
            scan(q, 0)
            lru(q)
        vec_items.append(finish)
        if q + 1 < nq:
            vec_items.append(functools.partial(coeffs, q + 1, 0))
    assert len(vec_items) == len(mxu_items)
    mlp_in()
    for vec, mxu in zip(vec_items, mxu_items):
        vec()
        mxu()
    mlp_out()
    merge()
    project()
    x1_s[...] = state.pop('x1')


def _bwd_mlp_call(uc, hf, gg, gl, gp, x, h0, mods, P, l, unperm, fg, *, tile, final_norm):
    bsz, seq, d = x.shape
    n_tiles = seq // tile
    n_steps = bsz * n_tiles

    def tile_of(s):
        s = jnp.clip(s, 0, n_steps - 1)
        return s // n_tiles, n_tiles - 1 - s % n_tiles

    def tok(shift):
        def index_map(s):
            b, j = tile_of(s - shift)
            return (b, j, 0)
        return pl.BlockSpec((1, tile, d), index_map)

    def mod_spec(shift):
        return pl.BlockSpec((1, 1, N_MOD, d), lambda s: (l, tile_of(s - shift)[0], 0, 0))

    args = [uc, hf, gg, gl, gp, x, P['wg'], P['br'], P['bi'], P['lam'], h0, mods, mods,
            P['wlo'], P['wo'], unperm, P['g2'], P['w1'], P['w2'], fg]
    in_specs = [tok(0)] * 6 + [
        _layer_spec(P['wg'], l, 1), _layer_spec(P['br'], l, 1), _layer_spec(P['bi'], l, 1),
        _layer_spec(P['lam'], l, 1),
        pl.BlockSpec((1, 1, D_LRU), lambda s: (tile_of(s)[0], 0, 0)),
        mod_spec(0), mod_spec(1),
        _layer_spec(P['wlo'], l), _layer_spec(P['wo'], l), _layer_spec(unperm),
        _layer_spec(P['g2'], l), _layer_spec(P['w1'], l), _layer_spec(P['w2'], l), _layer_spec(fg)]
    return pl.pallas_call(
        functools.partial(_bwd_mlp_kernel, tile=tile, n_tiles=n_tiles, final_norm=final_norm),
        grid=(n_steps + 1,),
        in_specs=in_specs,
        out_specs=tok(1),
        out_shape=jax.ShapeDtypeStruct((bsz, seq, d), F32),
        scratch_shapes=[
            pltpu.VMEM((tile, D_LRU), F32),
            pltpu.VMEM((tile, d), F32),
            pltpu.VMEM((tile, d), BF16),
            pltpu.VMEM((1, D_LRU), F32),
        ],
        compiler_params=pltpu.CompilerParams(
            dimension_semantics=("arbitrary",), vmem_limit_bytes=VMEM_LIMIT),
        name=f"bwdmlp_t{tile}",
    )(*args)


def _interleave_tables():
    r = np.arange(ROW_BLOCK)
    token = N_STEPS * (r % SUBLANES) + r // SUBLANES
    perm = np.zeros((ROW_BLOCK, ROW_BLOCK), np.float32)
    perm[r, token] = 1.0
    return token, jnp.asarray(perm, BF16), jnp.asarray(perm.T, BF16)


def _pool_tables(tile, row_len, token):
    assert ROW_BLOCK % row_len == 0 and tile % ROW_BLOCK == 0
    pos = token % row_len
    same_row = (token[:, None] // row_len) == (token[None, :] // row_len)
    mats, invs = [], []
    for w in POOL_WINDOWS:
        lo = np.clip(pos - w // 2, 0, row_len)
        hi = np.clip(pos + w - w // 2, 0, row_len)
        member = same_row & (pos[None, :] >= lo[:, None]) & (pos[None, :] < hi[:, None])
        mats.append(member.astype(np.float32))
        inv = np.repeat((1.0 / (hi - lo).astype(np.float32))[:, None], POOL_GROUP, axis=1)
        invs.append(np.tile(inv, (tile // ROW_BLOCK, 1)))
    return jnp.asarray(np.stack(mats), BF16), jnp.asarray(np.concatenate(invs, axis=1), F32)


def _gate_weights(w_r, w_i):
    hpg = GATE_GROUP // LRU_HEAD_DIM
    eye = jnp.eye(hpg, dtype=BF16)

    def blockdiag(w):
        w6 = w.astype(BF16).reshape(w.shape[0], 2, N_GATE_GROUPS, hpg, LRU_HEAD_DIM, LRU_HEAD_DIM)
        bd = jnp.einsum('ldqhie,hk->ldqhike', w6, eye)
        return bd.reshape(w.shape[0], 2, N_GATE_GROUPS, GATE_GROUP, GATE_GROUP)

    return jnp.concatenate([blockdiag(w_r), blockdiag(w_i)], axis=-1)


def kernel(x, c, ctx, c_ctx, w_ada, b_ada, norm1_g, norm2_g, w_in, conv_w, conv_b, lru_w_r, lru_b_r,
           lru_w_i, lru_b_i, lru_lambda, w_lru_out, pool_w, pool_scale, w_pool_out, w_o, mlp_w1,
           mlp_w2, final_g):
    bsz, seq, d = x.shape
    ctx_len = ctx.shape[1]
    depth = w_ada.shape[0]
    assert d == D_MODEL and seq % SEQ_TILE == 0
    assert bsz + 1 <= SUBLANES and ctx_len % HALO == 0

    cc = jnp.concatenate([c, c_ctx[None], jnp.zeros((SUBLANES - bsz - 1, d), F32)], axis=0)
    mods = _ada_call(cc, w_ada, b_ada).reshape(depth, SUBLANES, N_MOD, d)
    lat_row = lambda b: b
    ctx_row = lambda b: bsz

    token, perm, unperm = _interleave_tables()
    pm_lat, pinv_lat = _pool_tables(SEQ_TILE, GRID_W, token)
    pm_ctx, pinv_ctx = _pool_tables(ctx_len, ctx_len, token)
    zeros_state = jnp.zeros((bsz, 1, D_LRU), F32)
    fg = final_g.reshape(1, d)
    rows = lambda v: v.reshape(v.shape[:-1] + (1, v.shape[-1]))
    P = dict(
        g1=rows(norm1_g), g2=rows(norm2_g), w_in=w_in.astype(BF16), cw=conv_w, cb=rows(conv_b),
        wg=_gate_weights(lru_w_r, lru_w_i), br=rows(lru_b_r), bi=rows(lru_b_i),
        lam=rows(lru_lambda), pw=pool_w.astype(BF16), ps=rows(pool_scale),
        wpo=w_pool_out.astype(BF16), wlo=w_lru_out.astype(BF16), wo=w_o.astype(BF16),
        w1=mlp_w1.astype(BF16), w2=mlp_w2.astype(BF16))

    for l in range(depth):
        last = l == depth - 1
        if last:
            uc_c, hfin_f = _fwd_call(ctx, mods, ctx_row, zeros_state, P, l, perm, None, None,
                                     tile=ctx_len, full=False)
            hfin_b = _bwd_call(uc_c, zeros_state, P, l, tile=ctx_len, full=False)
        else:
            uc_c, hf_c, hfin_f, gg_c, gl_c, gp_c = _fwd_call(
                ctx, mods, ctx_row, zeros_state, P, l, perm, pm_ctx, pinv_ctx, tile=ctx_len,
                full=True)
            hfin_b, ctx = _bwd_call(uc_c, zeros_state, P, l, ctx, hf_c, gg_c, gl_c, gp_c, mods,
                                    ctx_row, unperm, tile=ctx_len, full=True)
            ctx = _mlp_call(ctx.reshape(1, bsz * ctx_len, d), mods, ctx_row, P, l, fg,
                            tile=bsz * ctx_len, final_norm=False).reshape(bsz, ctx_len, d)

        uc_l, hf_l, _, gg_l, gl_l, gp_l = _fwd_call(
            x, mods, lat_row, hfin_f, P, l, perm, pm_lat, pinv_lat, tile=SEQ_TILE, full=True)
        x = _bwd_mlp_call(uc_l, hf_l, gg_l, gl_l, gp_l, x, hfin_b, mods, P, l, unperm, fg,
                          tile=SEQ_TILE, final_norm=last)
    return x
```

```python
import functools

import jax
import jax.numpy as jnp
import numpy as np
from jax import lax
from jax.experimental import pallas as pl
from jax.experimental.pallas import tpu as pltpu

D_MODEL = 1024
D_LRU = 1024
LRU_HEADS = 16
LRU_HEAD_DIM = D_LRU // LRU_HEADS
CONV_WIDTH = 4
CONV_LEFT = 2
LRU_C = 8.0
D_POOL = 512
POOL_WINDOWS = (2, 4, 8, 16)
POOL_GROUP = D_POOL // len(POOL_WINDOWS)
D_FF = 4 * D_MODEL
N_MOD = 6
GRID_W = 64
EPS = 1e-6
LOG2_E = 1.4426950408889634

SUBLANES = 8
HALO = 16
GATE_GROUP = 256
N_GATE_GROUPS = D_LRU // GATE_GROUP
ROW_BLOCK = 256
N_STEPS = ROW_BLOCK // SUBLANES
SEQ_TILE = 512
FF_CHUNK = 1024
MLP_FUSED_CHUNK = 1024
ADA_CHUNK = 1536
VMEM_LIMIT = 56 * 1024 * 1024

F32 = jnp.float32
BF16 = jnp.bfloat16


def _layer_spec(arr, *lead):
    n_lead = len(lead)
    shape = (1,) * n_lead + arr.shape[n_lead:]
    idx = tuple(lead) + (0,) * (arr.ndim - n_lead)
    return pl.BlockSpec(shape, lambda *_: idx, pipeline_mode=pl.Buffered(1))


def _modulate(x, g, scale, shift):
    ms = jnp.mean(x * x, axis=-1, keepdims=True)
    return (x * lax.rsqrt(ms + EPS) * g) * (1.0 + scale) + shift


def _softplus(x):
    return jnp.maximum(x, 0.0) + jnp.log1p(jnp.exp(-jnp.abs(x)))


GELU_C1 = 0.7978845608028654
GELU_C2 = 0.044715 * GELU_C1


def _gelu_tanh(x):
    hx = 0.5 * x
    return hx + hx * jnp.tanh(x * (GELU_C1 + GELU_C2 * (x * x)))


def _ada_kernel(c_ref, w_ref, b_ref, o_ref):
    c = c_ref[...]
    s = c * jax.nn.sigmoid(c)
    o_ref[0] = jnp.dot(s, w_ref[0], preferred_element_type=F32,
                       precision=lax.Precision.HIGHEST) + b_ref[0]


def _ada_call(cc, w_ada, b_ada):
    depth, d, n = w_ada.shape
    return pl.pallas_call(
        _ada_kernel,
        grid=(depth, n // ADA_CHUNK),
        in_specs=[
            pl.BlockSpec((SUBLANES, d), lambda l, k: (0, 0)),
            pl.BlockSpec((1, d, ADA_CHUNK), lambda l, k: (l, 0, k)),
            pl.BlockSpec((1, 1, ADA_CHUNK), lambda l, k: (l, 0, k)),
        ],
        out_specs=pl.BlockSpec((1, SUBLANES, ADA_CHUNK), lambda l, k: (l, 0, k)),
        out_shape=jax.ShapeDtypeStruct((depth, SUBLANES, n), F32),
        compiler_params=pltpu.CompilerParams(
            dimension_semantics=("arbitrary", "arbitrary"), vmem_limit_bytes=VMEM_LIMIT),
        name="adaln",
    )(cc, w_ada, b_ada.reshape(depth, 1, n))


def _decay_rates(lam):
    sp = _softplus(-lam)
    return LRU_C * sp, (-LRU_C * LOG2_E) * sp


def _gate_coeffs(uc, wg, br, bi, nla_rate, l2a_rate):
    z = jnp.dot(uc.astype(BF16), wg, preferred_element_type=F32)
    r = jax.nn.sigmoid(z[:, :GATE_GROUP] + br)
    i = jax.nn.sigmoid(z[:, GATE_GROUP:] + bi)
    a = jnp.exp2(r * l2a_rate)
    t = jnp.tanh(r * nla_rate) * (1.0 + a * a)
    mult = jnp.where(t > 0.0, t * lax.rsqrt(t), 0.0)
    return a, mult * (i * uc)


def _group_scan(ag, bg, c, reverse):
    sub = lax.broadcasted_iota(jnp.int32, ag.shape, 0)
    for k in (1, 2, 4):
        shift = (SUBLANES - k) if reverse else k
        m = (sub < SUBLANES - k) if reverse else (sub >= k)
        ra = pltpu.roll(ag, shift, 0)
        rb = pltpu.roll(bg, shift, 0)
        bg = jnp.where(m, ag * rb, 0.0) + bg
        ag = jnp.where(m, ag * ra, ag)
    return ag * c + bg


def _scan_block(a, b, c, emit, reverse):
    sub = lax.broadcasted_iota(jnp.int32, (SUBLANES, a.shape[1]), 0)
    local, decay = {}, {}
    h = acc = None
    for k in (range(N_STEPS - 1, -1, -1) if reverse else range(N_STEPS)):
        rws = slice(k * SUBLANES, (k + 1) * SUBLANES)
        h = b[rws] if h is None else a[rws] * h + b[rws]
        acc = a[rws] if acc is None else a[rws] * acc
        local[k], decay[k] = h, acc
    ends = _group_scan(acc, h, c, reverse)
    if reverse:
        start = jnp.where(sub == SUBLANES - 1, c, pltpu.roll(ends, SUBLANES - 1, 0))
        c = ends[0:1]
    else:
        start = jnp.where(sub == 0, c, pltpu.roll(ends, 1, 0))
        c = ends[SUBLANES - 1:SUBLANES]
    for k in range(N_STEPS):
        emit(k * SUBLANES, local[k] + decay[k] * start)
    return c


def _scan_blocks(a, b, c, emit, reverse):
    n_blocks = a.shape[0] // ROW_BLOCK
    for blk in (range(n_blocks - 1, -1, -1) if reverse else range(n_blocks)):
        base = blk * ROW_BLOCK
        rws = slice(base, base + ROW_BLOCK)
        c = _scan_block(a[rws], b[rws], c, lambda r0, h, base=base: emit(base + r0, h), reverse)
    return c


def _group_cols():
    return [slice(q * GATE_GROUP, (q + 1) * GATE_GROUP) for q in range(N_GATE_GROUPS)]


def _to_token_order(v, unperm_ref):
    blocks = [jnp.dot(unperm_ref[...], v[r0:r0 + ROW_BLOCK], preferred_element_type=F32)
              for r0 in range(0, v.shape[0], ROW_BLOCK)]
    return jnp.concatenate(blocks, axis=0).astype(BF16)


def _fwd_kernel(*refs, tile, n_tiles, full):
    if full:
        (x_ref, xp_ref, xn_ref, mod_ref, g1_ref, win_ref, cw_ref, cb_ref, wg_ref, br_ref, bi_ref,
         lam_ref, h0_ref, perm_ref, pm_ref, pinv_ref, pw_ref, ps_ref, wpo_ref,
         uc_ref, hf_ref, hfin_ref, gg_ref, gl_ref, gp_ref, lhs_s, carry_s) = refs
    else:
        (x_ref, xp_ref, xn_ref, mod_ref, g1_ref, win_ref, cw_ref, cb_ref, wg_ref, br_ref, bi_ref,
         lam_ref, h0_ref, perm_ref, uc_ref, hfin_ref, lhs_s, carry_s) = refs
    j = pl.program_id(1)
    nq = N_GATE_GROUPS
    qcols = _group_cols()

    @pl.when(j == 0)
    def _():
        carry_s[...] = h0_ref[0]

    mod = mod_ref[0, 0]
    x_ext = jnp.concatenate([xp_ref[0], x_ref[0], xn_ref[0]], axis=0)
    hb_ext = _modulate(x_ext, g1_ref[0], mod[1:2], mod[0:1]).astype(BF16)
    n_blocks = tile // ROW_BLOCK
    blocks = [jnp.dot(perm_ref[...], hb_ext[HALO + blk * ROW_BLOCK:HALO + (blk + 1) * ROW_BLOCK],
                      preferred_element_type=F32).astype(BF16) for blk in range(n_blocks)]
    lhs_s[...] = jnp.concatenate(blocks + [hb_ext[0:HALO], hb_ext[HALO + tile:]], axis=0)
    keep_prev = jnp.where(j > 0, 1.0, 0.0)
    keep_next = jnp.where(j < n_tiles - 1, 1.0, 0.0)
    cw = cw_ref[0]
    cb = cb_ref[0]
    nla_rate, l2a_rate = _decay_rates(lam_ref[0, 0])
    br, bi = br_ref[0, 0], bi_ref[0, 0]
    o_g, o_p, o_t = D_LRU, 2 * D_LRU, 2 * D_LRU + D_POOL
    sub = lax.broadcasted_iota(jnp.int32, (SUBLANES, GATE_GROUP), 0)
    last = ROW_BLOCK - SUBLANES

    def w_cols(lo, width=GATE_GROUP):
        return win_ref[0, :, lo:lo + width]

    def conv(q):
        u = jnp.dot(lhs_s[...], w_cols(q * GATE_GROUP), preferred_element_type=F32)
        ublk = [u[blk * ROW_BLOCK:(blk + 1) * ROW_BLOCK] for blk in range(n_blocks)]
        before = u[tile + HALO - SUBLANES:tile + HALO] * keep_prev
        after = u[tile + HALO:tile + HALO + SUBLANES] * keep_next
        w = [cw[k:k + 1, qcols[q]] for k in range(CONV_WIDTH)]
        out = []
        for blk in range(n_blocks):
            cur = ublk[blk]
            if blk == 0:
                edge1, edge2 = pltpu.roll(before, 1, 0), pltpu.roll(before, 2, 0)
            else:
                edge1 = pltpu.roll(ublk[blk - 1][last:], 1, 0)
                edge2 = pltpu.roll(ublk[blk - 1][last - SUBLANES:last], 1, 0)
            nxt = after if blk == n_blocks - 1 else ublk[blk + 1][0:SUBLANES]
            head1 = jnp.where(sub == 0, edge1, pltpu.roll(cur[last:], 1, 0))
            head2 = jnp.where(sub == 0, edge2, pltpu.roll(cur[last - SUBLANES:last], 1, 0))
            tail = jnp.where(sub == SUBLANES - 1, pltpu.roll(nxt, SUBLANES - 1, 0),
                             pltpu.roll(cur[0:SUBLANES], SUBLANES - 1, 0))
            m1 = jnp.concatenate([head1, cur[:last]], axis=0)
            m2 = jnp.concatenate([head2, m1[:last]], axis=0)
            p1 = jnp.concatenate([cur[SUBLANES:], tail], axis=0)
            out.append(cb[:, qcols[q]] + w[0] * m2 + w[1] * m1 + w[2] * cur + w[3] * p1)
        uc = jnp.concatenate(out, axis=0)
        uc_ref[0, :, qcols[q]] = uc
        return uc

    def coeffs(q, uc, blk):
        rws = slice(blk * ROW_BLOCK, (blk + 1) * ROW_BLOCK)
        return _gate_coeffs(uc[rws], wg_ref[0, 0, q], br[:, qcols[q]], bi[:, qcols[q]],
                            nla_rate[:, qcols[q]], l2a_rate[:, qcols[q]])

    def scan(q, ab):
        c = carry_s[:, qcols[q]]
        for blk, (a, b) in enumerate(ab):
            if full:
                def emit(r0, h, base=blk * ROW_BLOCK):
                    hf_ref[0, base + r0:base + r0 + SUBLANES, qcols[q]] = h
            else:
                def emit(r0, h):
                    del r0, h
            c = _scan_block(a, b, c, emit, reverse=False)
        carry_s[:, qcols[q]] = c
        hfin_ref[0, :, qcols[q]] = c

    if not full:
        for q in range(nq):
            uc = conv(q)
            scan(q, [coeffs(q, uc, blk) for blk in range(n_blocks)])
        return

    def pool_mean(gi, p, p_hi, p_lo):
        cols = slice(gi * POOL_GROUP, (gi + 1) * POOL_GROUP)
        wsum = []
        for r0 in range(0, tile, ROW_BLOCK):
            rws = slice(r0, r0 + ROW_BLOCK)
            rhs = jnp.concatenate([p_hi[rws, cols], p_lo[rws, cols]], axis=1)
            s2 = jnp.dot(pm_ref[gi], rhs, preferred_element_type=F32)
            wsum.append(s2[:, :POOL_GROUP] + s2[:, POOL_GROUP:])
        return (jnp.concatenate(wsum, axis=0) * pinv_ref[:, cols] - p[:, cols]).astype(BF16)

    def pool_mix(gi, m):
        return jnp.dot(m, pw_ref[0, gi], preferred_element_type=F32)

    def gelu_cols(q):
        g = jnp.dot(lhs_s[0:tile], w_cols(o_g + q * GATE_GROUP), preferred_element_type=F32)
        gg_ref[0, :, qcols[q]] = _gelu_tanh(g).astype(BF16)

    def lru_gate_cols(q):
        gt = jnp.dot(lhs_s[0:tile], w_cols(o_t + q * GATE_GROUP), preferred_element_type=F32)
        gl_ref[0, :, qcols[q]] = jax.nn.sigmoid(gt).astype(BF16)

    def pool_gate_cols(q, yb):
        gt = jnp.dot(lhs_s[0:tile], w_cols(o_t + D_MODEL + q * GATE_GROUP),
                     preferred_element_type=F32)
        po = jnp.dot(yb, wpo_ref[0, :, qcols[q]], preferred_element_type=F32)
        gp_ref[0, :, qcols[q]] = (jax.nn.sigmoid(gt) * po).astype(BF16)

    ucs = [conv(0), conv(1)]
    p = jnp.dot(lhs_s[0:tile], w_cols(o_p, D_POOL), preferred_element_type=F32)
    p_hi = p.astype(BF16)
    p_lo = (p - p_hi.astype(F32)).astype(BF16)
    ucs.append(conv(2))
    ys = [pool_mix(gi, pool_mean(gi, p, p_hi, p_lo)) for gi in (0, 1)]
    ucs.append(conv(3))
    ab = [coeffs(0, ucs[0], blk) for blk in range(n_blocks)]
    ys += [pool_mix(gi, pool_mean(gi, p, p_hi, p_lo)) for gi in (2, 3)]
    yb = (jnp.concatenate(ys, axis=-1) * ps_ref[0]).astype(BF16)
    for q in range(nq):
        nxt = []
        gelu_cols(q)
        if q + 1 < nq:
            nxt.append(coeffs(q + 1, ucs[q + 1], 0))
        lru_gate_cols(q)
        if q + 1 < nq:
            nxt += [coeffs(q + 1, ucs[q + 1], blk) for blk in range(1, n_blocks)]
        scan(q, ab)
        pool_gate_cols(q, yb)
        ab = nxt


def _fwd_call(x, mods, mod_row, h0, P, l, perm, pm, pinv, *, tile, full):
    bsz, seq, d = x.shape
    n_tiles = seq // tile
    hb = tile // HALO
    n_hb = seq // HALO
    tok = lambda b, j: (b, j, 0)
    tok_spec = pl.BlockSpec((1, tile, d), tok)
    state = pl.BlockSpec((1, 1, D_LRU), lambda b, j: (b, 0, 0))
    args = [x, x, x, mods, P['g1'], P['w_in'], P['cw'], P['cb'], P['wg'], P['br'], P['bi'],
            P['lam'], h0, perm]
    in_specs = [
        tok_spec,
        pl.BlockSpec((1, HALO, d), lambda b, j: (b, jnp.maximum(j * hb - 1, 0), 0)),
        pl.BlockSpec((1, HALO, d), lambda b, j: (b, jnp.minimum((j + 1) * hb, n_hb - 1), 0)),
        pl.BlockSpec((1, 1, N_MOD, d), lambda b, j: (l, mod_row(b), 0, 0)),
        _layer_spec(P['g1'], l), _layer_spec(P['w_in'], l), _layer_spec(P['cw'], l),
        _layer_spec(P['cb'], l), _layer_spec(P['wg'], l, 0), _layer_spec(P['br'], l, 0),
        _layer_spec(P['bi'], l, 0), _layer_spec(P['lam'], l, 0), state, _layer_spec(perm),
    ]
    f32_tok = jax.ShapeDtypeStruct((bsz, seq, D_LRU), F32)
    bf_tok = jax.ShapeDtypeStruct((bsz, seq, D_LRU), BF16)
    st = jax.ShapeDtypeStruct((bsz, 1, D_LRU), F32)
    if full:
        args += [pm, pinv, P['pw'], P['ps'], P['wpo']]
        in_specs += [_layer_spec(pm), _layer_spec(pinv), _layer_spec(P['pw'], l),
                     _layer_spec(P['ps'], l), _layer_spec(P['wpo'], l)]
        out_specs = [tok_spec, tok_spec, state, tok_spec, tok_spec, tok_spec]
        out_shape = [f32_tok, f32_tok, st, bf_tok, bf_tok, bf_tok]
    else:
        out_specs = [tok_spec, state]
        out_shape = [f32_tok, st]
    return pl.pallas_call(
        functools.partial(_fwd_kernel, tile=tile, n_tiles=n_tiles, full=full),
        grid=(bsz, n_tiles),
        in_specs=in_specs,
        out_specs=out_specs,
        out_shape=out_shape,
        scratch_shapes=[pltpu.VMEM((tile + 2 * HALO, d), BF16), pltpu.VMEM((1, D_LRU), F32)],
        compiler_params=pltpu.CompilerParams(
            dimension_semantics=("arbitrary", "arbitrary"), vmem_limit_bytes=VMEM_LIMIT),
        name=f"fwd_t{tile}" if full else f"fwdstate_t{tile}",
    )(*args)


def _bwd_kernel(*refs, tile, full):
    if full:
        (uc_ref, wg_ref, br_ref, bi_ref, lam_ref, h0_ref, x_ref, hf_ref, gg_ref, gl_ref, gp_ref,
         mod_ref, wlo_ref, wo_ref, unperm_ref, hfin_ref, out_ref, hl_s, carry_s) = refs
    else:
        (uc_ref, wg_ref, br_ref, bi_ref, lam_ref, h0_ref, hfin_ref, carry_s) = refs
    j = pl.program_id(1)
    nq = N_GATE_GROUPS
    qcols = _group_cols()

    @pl.when(j == 0)
    def _():
        carry_s[...] = h0_ref[0]

    nla_rate, l2a_rate = _decay_rates(lam_ref[0, 0])
    br, bi = br_ref[0, 0], bi_ref[0, 0]

    def coeffs(q):
        return _gate_coeffs(uc_ref[0, :, qcols[q]], wg_ref[0, 0, q], br[:, qcols[q]],
                            bi[:, qcols[q]], nla_rate[:, qcols[q]], l2a_rate[:, qcols[q]])

    def scan(q, ab):
        if full:
            def emit(r0, h):
                rws = slice(r0, r0 + SUBLANES)
                hl_s[rws, qcols[q]] = h + hf_ref[0, rws, qcols[q]]
        else:
            def emit(r0, h):
                del r0, h
        c = _scan_blocks(ab[0], ab[1], carry_s[:, qcols[q]], emit, reverse=True)
        carry_s[:, qcols[q]] = c
        hfin_ref[0, :, qcols[q]] = c

    ab = coeffs(0)
    lru_out = None
    for q in range(nq):
        nxt = coeffs(q + 1) if q + 1 < nq else None
        scan(q, ab)
        ab = nxt
        if full:
            lru_in = (hl_s[:, qcols[q]] * gg_ref[0, :, qcols[q]].astype(F32)).astype(BF16)
            part = jnp.dot(lru_in, wlo_ref[0, qcols[q], :], preferred_element_type=F32)
            lru_out = part if lru_out is None else lru_out + part
    if not full:
        return

    merged = gl_ref[0].astype(F32) * lru_out + gp_ref[0].astype(F32)
    yo = jnp.dot(_to_token_order(merged.astype(BF16), unperm_ref), wo_ref[0],
                 preferred_element_type=F32)
    out_ref[0] = x_ref[0] + mod_ref[0, 0][2:3] * yo


def _bwd_call(uc, h0, P, l, x=None, hf=None, gg=None, gl=None, gp=None, mods=None, mod_row=None,
              unperm=None, *, tile, full):
    bsz, seq, d = uc.shape
    n_tiles = seq // tile
    tok = lambda b, j: (b, n_tiles - 1 - j, 0)
    tok_spec = pl.BlockSpec((1, tile, d), tok)
    state = pl.BlockSpec((1, 1, D_LRU), lambda b, j: (b, 0, 0))
    args = [uc, P['wg'], P['br'], P['bi'], P['lam'], h0]
    in_specs = [tok_spec, _layer_spec(P['wg'], l, 1), _layer_spec(P['br'], l, 1),
                _layer_spec(P['bi'], l, 1), _layer_spec(P['lam'], l, 1), state]
    st = jax.ShapeDtypeStruct((bsz, 1, D_LRU), F32)
    scratch = [pltpu.VMEM((1, D_LRU), F32)]
    if full:
        args += [x, hf, gg, gl, gp, mods, P['wlo'], P['wo'], unperm]
        in_specs += [tok_spec] * 5 + [
            pl.BlockSpec((1, 1, N_MOD, d), lambda b, j: (l, mod_row(b), 0, 0)),
            _layer_spec(P['wlo'], l), _layer_spec(P['wo'], l), _layer_spec(unperm)]
        out_specs = [state, tok_spec]
        out_shape = [st, jax.ShapeDtypeStruct((bsz, seq, d), F32)]
        scratch = [pltpu.VMEM((tile, D_LRU), F32)] + scratch
    else:
        out_specs = state
        out_shape = st
    return pl.pallas_call(
        functools.partial(_bwd_kernel, tile=tile, full=full),
        grid=(bsz, n_tiles),
        in_specs=in_specs,
        out_specs=out_specs,
        out_shape=out_shape,
        scratch_shapes=scratch,
        compiler_params=pltpu.CompilerParams(
            dimension_semantics=("arbitrary", "arbitrary"), vmem_limit_bytes=VMEM_LIMIT),
        name=f"bwd_t{tile}" if full else f"bwdstate_t{tile}",
    )(*args)


def _mlp_kernel(x_ref, mod_ref, g2_ref, w1_ref, w2_ref, fg_ref, out_ref, hm_s, acc_s, *,
                n_chunks, final_norm):
    k = pl.program_id(2)

    @pl.when(k == 0)
    def _():
        mod = mod_ref[0, 0]
        hm_s[...] = _modulate(x_ref[0], g2_ref[0], mod[4:5], mod[3:4]).astype(BF16)
        acc_s[...] = jnp.zeros_like(acc_s)

    hid = jnp.dot(hm_s[...], w1_ref[0], preferred_element_type=F32)
    hid = jnp.square(jnp.maximum(hid, 0.0))
    acc_s[...] += jnp.dot(hid.astype(BF16), w2_ref[0], preferred_element_type=F32)

    @pl.when(k == n_chunks - 1)
    def _():
        y = x_ref[0] + mod_ref[0, 0][5:6] * acc_s[...]
        if final_norm:
            ms = jnp.mean(y * y, axis=-1, keepdims=True)
            y = y * lax.rsqrt(ms + EPS) * fg_ref[...]
        out_ref[0] = y


def _mlp_call(x, mods, mod_row, P, l, fg, *, tile, final_norm):
    bsz, seq, d = x.shape
    n_chunks = D_FF // FF_CHUNK
    tok = pl.BlockSpec((1, tile, d), lambda b, j, k: (b, j, 0))
    return pl.pallas_call(
        functools.partial(_mlp_kernel, n_chunks=n_chunks, final_norm=final_norm),
        grid=(bsz, seq // tile, n_chunks),
        in_specs=[
            tok,
            pl.BlockSpec((1, 1, N_MOD, d), lambda b, j, k: (l, mod_row(b), 0, 0)),
            _layer_spec(P['g2'], l),
            pl.BlockSpec((1, d, FF_CHUNK), lambda b, j, k: (l, 0, k)),
            pl.BlockSpec((1, FF_CHUNK, d), lambda b, j, k: (l, k, 0)),
            _layer_spec(fg),
        ],
        out_specs=tok,
        out_shape=jax.ShapeDtypeStruct((bsz, seq, d), F32),
        scratch_shapes=[pltpu.VMEM((tile, d), BF16), pltpu.VMEM((tile, d), F32)],
        compiler_params=pltpu.CompilerParams(
            dimension_semantics=("arbitrary", "arbitrary", "arbitrary"),
            vmem_limit_bytes=VMEM_LIMIT),
        name=f"mlp_t{tile}",
    )(x, mods, P['g2'], P['w1'], P['w2'], fg)


def _bwd_mlp_kernel(uc_ref, hf_ref, gg_ref, gl_ref, gp_ref, x_ref, wg_ref, br_ref, bi_ref, lam_ref,
                    h0_ref, mod_mix_ref, mod_mlp_ref, wlo_ref, wo_ref, unperm_ref, g2_ref, w1_ref,
                    w2_ref, fg_ref, out_ref, hl_s, x1_s, hm_s, carry_s, *, tile, n_tiles,
                    final_norm):
    s = pl.program_id(0)
    n_steps = pl.num_programs(0) - 1

    @pl.when(s % n_tiles == 0)
    def _():
        carry_s[...] = h0_ref[0]

    args = (uc_ref, hf_ref, gg_ref, gl_ref, gp_ref, x_ref, wg_ref, br_ref, bi_ref, lam_ref,
            mod_mix_ref, mod_mlp_ref, wlo_ref, wo_ref, unperm_ref, g2_ref, w1_ref, w2_ref, fg_ref,
            out_ref, hl_s, x1_s, hm_s, carry_s)
    step = functools.partial(_bwd_mlp_step, *args, tile=tile, final_norm=final_norm)
    pl.when(s == 0)(functools.partial(step, mixer=True, mlp=False))
    pl.when((s > 0) & (s < n_steps))(functools.partial(step, mixer=True, mlp=True))
    pl.when(s == n_steps)(functools.partial(step, mixer=False, mlp=True))


def _bwd_mlp_step(uc_ref, hf_ref, gg_ref, gl_ref, gp_ref, x_ref, wg_ref, br_ref, bi_ref, lam_ref,
                  mod_mix_ref, mod_mlp_ref, wlo_ref, wo_ref, unperm_ref, g2_ref, w1_ref, w2_ref,
                  fg_ref, out_ref, hl_s, x1_s, hm_s, carry_s, *, tile, final_norm, mixer, mlp):
    nq = N_GATE_GROUPS
    qcols = _group_cols()

    nla_rate, l2a_rate = _decay_rates(lam_ref[0, 0])
    br, bi = br_ref[0, 0], bi_ref[0, 0]
    state = {}

    half = ROW_BLOCK
    assert tile == 2 * half

    def coeffs(q, h):
        rws = slice(h * half, (h + 1) * half)
        state['ab', q, h] = _gate_coeffs(
            uc_ref[0, rws, qcols[q]], wg_ref[0, 0, q], br[:, qcols[q]], bi[:, qcols[q]],
            nla_rate[:, qcols[q]], l2a_rate[:, qcols[q]])

    def scan(q, h):
        def emit(r0, hrows):
            rws = slice(h * half + r0, h * half + r0 + SUBLANES)
            hl_s[rws, qcols[q]] = hrows + hf_ref[0, rws, qcols[q]]
        a, b = state.pop(('ab', q, h))
        carry_s[:, qcols[q]] = _scan_block(a, b, carry_s[:, qcols[q]], emit, reverse=True)

    def lru(q):
        lru_in = (hl_s[:, qcols[q]] * gg_ref[0, :, qcols[q]].astype(F32)).astype(BF16)
        part = jnp.dot(lru_in, wlo_ref[0, qcols[q], :], preferred_element_type=F32)
        state['lru'] = part if q == 0 else state['lru'] + part

    def merge():
        merged = gl_ref[0].astype(F32) * state.pop('lru') + gp_ref[0].astype(F32)
        state['merged'] = _to_token_order(merged.astype(BF16), unperm_ref)

    def project():
        yo = jnp.dot(state.pop('merged'), wo_ref[0], preferred_element_type=F32)
        state['x1'] = x_ref[0] + mod_mix_ref[0, 0][2:3] * yo

    mod = mod_mlp_ref[0, 0]

    def mlp_in():
        hm_s[...] = _modulate(x1_s[...], g2_ref[0], mod[4:5], mod[3:4]).astype(BF16)

    def mlp_up(k):
        cols = slice(k * MLP_FUSED_CHUNK, (k + 1) * MLP_FUSED_CHUNK)
        hid = jnp.dot(hm_s[...], w1_ref[0, :, cols], preferred_element_type=F32)
        state['hid', k] = jnp.square(jnp.maximum(hid, 0.0)).astype(BF16)

    def mlp_down(k):
        cols = slice(k * MLP_FUSED_CHUNK, (k + 1) * MLP_FUSED_CHUNK)
        part = jnp.dot(state.pop(('hid', k)), w2_ref[0, cols, :], preferred_element_type=F32)
        state['acc'] = part if k == 0 else state['acc'] + part

    def mlp_out():
        y = x1_s[...] + mod[5:6] * state.pop('acc')
        if final_norm:
            ms = jnp.mean(y * y, axis=-1, keepdims=True)
            y = y * lax.rsqrt(ms + EPS) * fg_ref[...]
        out_ref[0] = y

    n_chunks = D_FF // MLP_FUSED_CHUNK
    mxu_items = [lambda: mlp_up(0)]
    for k in range(n_chunks):
        if k + 1 < n_chunks:
            mxu_items.append(functools.partial(mlp_up, k + 1))
        mxu_items.append(functools.partial(mlp_down, k))
    vec_items = [lambda: coeffs(0, 1), lambda: coeffs(0, 0)]
    for q in range(nq):
        vec_items.append(functools.partial(scan, q, 1))
        if q + 1 < nq:
            vec_items.append(functools.partial(coeffs, q + 1, 1))

        def finish(q=q):
            scan(q, 0)
            lru(q)
        vec_items.append(finish)
        if q + 1 < nq:
            vec_items.append(functools.partial(coeffs, q + 1, 0))
    if mixer and mlp:
        per_mxu, rem = divmod(len(vec_items), len(mxu_items))
        assert rem == 0
        mlp_in()
        for i, mxu in enumerate(mxu_items):
            for vec in vec_items[i * per_mxu:(i + 1) * per_mxu]:
                vec()
            mxu()
        mlp_out()
    elif mlp:
        mlp_in()
        for mxu in mxu_items:
            mxu()
        mlp_out()
    else:
        for vec in vec_items:
            vec()
    if mixer:
        merge()
        project()
        x1_s[...] = state.pop('x1')


def _bwd_mlp_call(uc, hf, gg, gl, gp, x, h0, mods, P, l, unperm, fg, *, tile, final_norm):
    bsz, seq, d = x.shape
    n_tiles = seq // tile
    n_steps = bsz * n_tiles

    def tile_of(s):
        s = jnp.clip(s, 0, n_steps - 1)
        return s // n_tiles, n_tiles - 1 - s % n_tiles

    def tok(shift):
        def index_map(s):
            b, j = tile_of(s - shift)
            return (b, j, 0)
        return pl.BlockSpec((1, tile, d), index_map)

    def mod_spec(shift):
        return pl.BlockSpec((1, 1, N_MOD, d), lambda s: (l, tile_of(s - shift)[0], 0, 0))

    args = [uc, hf, gg, gl, gp, x, P['wg'], P['br'], P['bi'], P['lam'], h0, mods, mods,
            P['wlo'], P['wo'], unperm, P['g2'], P['w1'], P['w2'], fg]
    in_specs = [tok(0)] * 6 + [
        _layer_spec(P['wg'], l, 1), _layer_spec(P['br'], l, 1), _layer_spec(P['bi'], l, 1),
        _layer_spec(P['lam'], l, 1),
        pl.BlockSpec((1, 1, D_LRU), lambda s: (tile_of(s)[0], 0, 0)),
        mod_spec(0), mod_spec(1),
        _layer_spec(P['wlo'], l), _layer_spec(P['wo'], l), _layer_spec(unperm),
        _layer_spec(P['g2'], l), _layer_spec(P['w1'], l), _layer_spec(P['w2'], l), _layer_spec(fg)]
    return pl.pallas_call(
        functools.partial(_bwd_mlp_kernel, tile=tile, n_tiles=n_tiles, final_norm=final_norm),
        grid=(n_steps + 1,),
        in_specs=in_specs,
        out_specs=tok(1),
        out_shape=jax.ShapeDtypeStruct((bsz, seq, d), F32),
        scratch_shapes=[
            pltpu.VMEM((tile, D_LRU), F32),
            pltpu.VMEM((tile, d), F32),
            pltpu.VMEM((tile, d), BF16),
            pltpu.VMEM((1, D_LRU), F32),
        ],
        compiler_params=pltpu.CompilerParams(
            dimension_semantics=("arbitrary",), vmem_limit_bytes=VMEM_LIMIT),
        name=f"bwdmlp_t{tile}",
    )(*args)


def _interleave_tables():
    r = np.arange(ROW_BLOCK)
    token = N_STEPS * (r % SUBLANES) + r // SUBLANES
    perm = np.zeros((ROW_BLOCK, ROW_BLOCK), np.float32)
    perm[r, token] = 1.0
    return token, jnp.asarray(perm, BF16), jnp.asarray(perm.T, BF16)


def _pool_tables(tile, row_len, token):
    assert ROW_BLOCK % row_len == 0 and tile % ROW_BLOCK == 0
    pos = token % row_len
    same_row = (token[:, None] // row_len) == (token[None, :] // row_len)
    mats, invs = [], []
    for w in POOL_WINDOWS:
        lo = np.clip(pos - w // 2, 0, row_len)
        hi = np.clip(pos + w - w // 2, 0, row_len)
        member = same_row & (pos[None, :] >= lo[:, None]) & (pos[None, :] < hi[:, None])
        mats.append(member.astype(np.float32))
        inv = np.repeat((1.0 / (hi - lo).astype(np.float32))[:, None], POOL_GROUP, axis=1)
        invs.append(np.tile(inv, (tile // ROW_BLOCK, 1)))
    return jnp.asarray(np.stack(mats), BF16), jnp.asarray(np.concatenate(invs, axis=1), F32)


def _gate_weights(w_r, w_i):
    hpg = GATE_GROUP // LRU_HEAD_DIM
    eye = jnp.eye(hpg, dtype=BF16)

    def blockdiag(w):
        w6 = w.astype(BF16).reshape(w.shape[0], 2, N_GATE_GROUPS, hpg, LRU_HEAD_DIM, LRU_HEAD_DIM)
        bd = jnp.einsum('ldqhie,hk->ldqhike', w6, eye)
        return bd.reshape(w.shape[0], 2, N_GATE_GROUPS, GATE_GROUP, GATE_GROUP)

    return jnp.concatenate([blockdiag(w_r), blockdiag(w_i)], axis=-1)


def kernel(x, c, ctx, c_ctx, w_ada, b_ada, norm1_g, norm2_g, w_in, conv_w, conv_b, lru_w_r, lru_b_r,
           lru_w_i, lru_b_i, lru_lambda, w_lru_out, pool_w, pool_scale, w_pool_out, w_o, mlp_w1,
           mlp_w2, final_g):
    bsz, seq, d = x.shape
    ctx_len = ctx.shape[1]
    depth = w_ada.shape[0]
    assert d == D_MODEL and seq % SEQ_TILE == 0
    assert bsz + 1 <= SUBLANES and ctx_len % HALO == 0

    cc = jnp.concatenate([c, c_ctx[None], jnp.zeros((SUBLANES - bsz - 1, d), F32)], axis=0)
    mods = _ada_call(cc, w_ada, b_ada).reshape(depth, SUBLANES, N_MOD, d)
    lat_row = lambda b: b
    ctx_row = lambda b: bsz

    token, perm, unperm = _interleave_tables()
    pm_lat, pinv_lat = _pool_tables(SEQ_TILE, GRID_W, token)
    pm_ctx, pinv_ctx = _pool_tables(ctx_len, ctx_len, token)
    zeros_state = jnp.zeros((bsz, 1, D_LRU), F32)
    fg = final_g.reshape(1, d)
    rows = lambda v: v.reshape(v.shape[:-1] + (1, v.shape[-1]))
    P = dict(
        g1=rows(norm1_g), g2=rows(norm2_g), w_in=w_in.astype(BF16), cw=conv_w, cb=rows(conv_b),
        wg=_gate_weights(lru_w_r, lru_w_i), br=rows(lru_b_r), bi=rows(lru_b_i),
        lam=rows(lru_lambda), pw=pool_w.astype(BF16), ps=rows(pool_scale),
        wpo=w_pool_out.astype(BF16), wlo=w_lru_out.astype(BF16), wo=w_o.astype(BF16),
        w1=mlp_w1.astype(BF16), w2=mlp_w2.astype(BF16))

    for l in range(depth):
        last = l == depth - 1
        if last:
            uc_c, hfin_f = _fwd_call(ctx, mods, ctx_row, zeros_state, P, l, perm, None, None,
                                     tile=ctx_len, full=False)
            hfin_b = _bwd_call(uc_c, zeros_state, P, l, tile=ctx_len, full=False)
        else:
            uc_c, hf_c, hfin_f, gg_c, gl_c, gp_c = _fwd_call(
                ctx, mods, ctx_row, zeros_state, P, l, perm, pm_ctx, pinv_ctx, tile=ctx_len,
                full=True)
            hfin_b, ctx = _bwd_call(uc_c, zeros_state, P, l, ctx, hf_c, gg_c, gl_c, gp_c, mods,
                                    ctx_row, unperm, tile=ctx_len, full=True)
            ctx = _mlp_call(ctx.reshape(1, bsz * ctx_len, d), mods, ctx_row, P, l, fg,
                            tile=bsz * ctx_len, final_norm=False).reshape(bsz, ctx_len, d)

        uc_l, hf_l, _, gg_l, gl_l, gp_l = _fwd_call(
            x, mods, lat_row, hfin_f, P, l, perm, pm_lat, pinv_lat, tile=SEQ_TILE, full=True)
        x = _bwd_mlp_call(uc_l, hf_l, gg_l, gl_l, gp_l, x, hfin_b, mods, P, l, unperm, fg,
                          tile=SEQ_TILE, final_norm=last)
    return x
```

```python
import functools

import jax
import jax.numpy as jnp
import numpy as np
from jax import lax
from jax.experimental import pallas as pl
from jax.experimental.pallas import tpu as pltpu

D_MODEL = 1024
D_LRU = 1024
LRU_HEADS = 16
LRU_HEAD_DIM = D_LRU // LRU_HEADS
CONV_WIDTH = 4
CONV_LEFT = 2
LRU_C = 8.0
D_POOL = 512
POOL_WINDOWS = (2, 4, 8, 16)
POOL_GROUP = D_POOL // len(POOL_WINDOWS)
D_FF = 4 * D_MODEL
N_MOD = 6
GRID_W = 64
EPS = 1e-6
LOG2_E = 1.4426950408889634

SUBLANES = 8
HALO = 16
GATE_GROUP = 256
N_GATE_GROUPS = D_LRU // GATE_GROUP
ROW_BLOCK = 256
N_STEPS = ROW_BLOCK // SUBLANES
SEQ_TILE = 512
FF_CHUNK = 1024
MLP_FUSED_CHUNK = 1024
ADA_CHUNK = 1536
VMEM_LIMIT = 56 * 1024 * 1024

F32 = jnp.float32
BF16 = jnp.bfloat16


def _layer_spec(arr, *lead):
    n_lead = len(lead)
    shape = (1,) * n_lead + arr.shape[n_lead:]
    idx = tuple(lead) + (0,) * (arr.ndim - n_lead)
    return pl.BlockSpec(shape, lambda *_: idx, pipeline_mode=pl.Buffered(1))


def _modulate(x, g, scale, shift):
    ms = jnp.mean(x * x, axis=-1, keepdims=True)
    return (x * lax.rsqrt(ms + EPS) * g) * (1.0 + scale) + shift


def _softplus(x):
    return jnp.maximum(x, 0.0) + jnp.log1p(jnp.exp(-jnp.abs(x)))


GELU_C1 = 0.7978845608028654
GELU_C2 = 0.044715 * GELU_C1


def _gelu_tanh(x):
    hx = 0.5 * x
    return hx + hx * jnp.tanh(x * (GELU_C1 + GELU_C2 * (x * x)))


def _ada_kernel(c_ref, w_ref, b_ref, o_ref):
    c = c_ref[...]
    s = c * jax.nn.sigmoid(c)
    o_ref[0] = jnp.dot(s, w_ref[0], preferred_element_type=F32,
                       precision=lax.Precision.HIGHEST) + b_ref[0]


def _ada_call(cc, w_ada, b_ada):
    depth, d, n = w_ada.shape
    return pl.pallas_call(
        _ada_kernel,
        grid=(depth, n // ADA_CHUNK),
        in_specs=[
            pl.BlockSpec((SUBLANES, d), lambda l, k: (0, 0)),
            pl.BlockSpec((1, d, ADA_CHUNK), lambda l, k: (l, 0, k)),
            pl.BlockSpec((1, 1, ADA_CHUNK), lambda l, k: (l, 0, k)),
        ],
        out_specs=pl.BlockSpec((1, SUBLANES, ADA_CHUNK), lambda l, k: (l, 0, k)),
        out_shape=jax.ShapeDtypeStruct((depth, SUBLANES, n), F32),
        compiler_params=pltpu.CompilerParams(
            dimension_semantics=("arbitrary", "arbitrary"), vmem_limit_bytes=VMEM_LIMIT),
        name="adaln",
    )(cc, w_ada, b_ada.reshape(depth, 1, n))


def _decay_rates(lam):
    sp = _softplus(-lam)
    return LRU_C * sp, (-LRU_C * LOG2_E) * sp


def _gate_coeffs(uc, wg, br, bi, nla_rate, l2a_rate):
    z = jnp.dot(uc.astype(BF16), wg, preferred_element_type=F32)
    r = jax.nn.sigmoid(z[:, :GATE_GROUP] + br)
    i = jax.nn.sigmoid(z[:, GATE_GROUP:] + bi)
    a = jnp.exp2(r * l2a_rate)
    t = jnp.tanh(r * nla_rate) * (1.0 + a * a)
    mult = jnp.where(t > 0.0, t * lax.rsqrt(t), 0.0)
    return a, mult * (i * uc.astype(F32))


def _group_scan(ag, bg, c, reverse):
    sub = lax.broadcasted_iota(jnp.int32, ag.shape, 0)
    for k in (1, 2, 4):
        shift = (SUBLANES - k) if reverse else k
        m = (sub < SUBLANES - k) if reverse else (sub >= k)
        ra = pltpu.roll(ag, shift, 0)
        rb = pltpu.roll(bg, shift, 0)
        bg = jnp.where(m, ag * rb, 0.0) + bg
        ag = jnp.where(m, ag * ra, ag)
    return ag * c + bg


def _scan_block(a, b, c, emit, reverse):
    sub = lax.broadcasted_iota(jnp.int32, (SUBLANES, a.shape[1]), 0)
    local, decay = {}, {}
    h = acc = None
    for k in (range(N_STEPS - 1, -1, -1) if reverse else range(N_STEPS)):
        rws = slice(k * SUBLANES, (k + 1) * SUBLANES)
        h = b[rws] if h is None else a[rws] * h + b[rws]
        acc = a[rws] if acc is None else a[rws] * acc
        local[k], decay[k] = h, acc
    ends = _group_scan(acc, h, c, reverse)
    if reverse:
        start = jnp.where(sub == SUBLANES - 1, c, pltpu.roll(ends, SUBLANES - 1, 0))
        c = ends[0:1]
    else:
        start = jnp.where(sub == 0, c, pltpu.roll(ends, 1, 0))
        c = ends[SUBLANES - 1:SUBLANES]
    for k in range(0, N_STEPS, 2):
        rows = [local[k + i] + decay[k + i] * start for i in range(2)]
        emit(k * SUBLANES, jnp.concatenate(rows, axis=0))
    return c


def _scan_blocks(a, b, c, emit, reverse):
    n_blocks = a.shape[0] // ROW_BLOCK
    for blk in (range(n_blocks - 1, -1, -1) if reverse else range(n_blocks)):
        base = blk * ROW_BLOCK
        rws = slice(base, base + ROW_BLOCK)
        c = _scan_block(a[rws], b[rws], c, lambda r0, h, base=base: emit(base + r0, h), reverse)
    return c


def _group_cols():
    return [slice(q * GATE_GROUP, (q + 1) * GATE_GROUP) for q in range(N_GATE_GROUPS)]


def _to_token_order(v, unperm_ref):
    blocks = [jnp.dot(unperm_ref[...], v[r0:r0 + ROW_BLOCK], preferred_element_type=F32)
              for r0 in range(0, v.shape[0], ROW_BLOCK)]
    return jnp.concatenate(blocks, axis=0).astype(BF16)


def _fwd_kernel(*refs, tile, n_tiles, full):
    if full:
        (x_ref, xp_ref, xn_ref, mod_ref, g1_ref, win_ref, cw_ref, cb_ref, wg_ref, br_ref, bi_ref,
         lam_ref, h0_ref, perm_ref, pm_ref, pinv_ref, pw_ref, ps_ref, wpo_ref,
         uc_ref, hf_ref, hfin_ref, gg_ref, gl_ref, gp_ref, lhs_s, carry_s) = refs
    else:
        (x_ref, xp_ref, xn_ref, mod_ref, g1_ref, win_ref, cw_ref, cb_ref, wg_ref, br_ref, bi_ref,
         lam_ref, h0_ref, perm_ref, uc_ref, hfin_ref, lhs_s, carry_s) = refs
    j = pl.program_id(1)
    nq = N_GATE_GROUPS
    qcols = _group_cols()

    @pl.when(j == 0)
    def _():
        carry_s[...] = h0_ref[0]

    mod = mod_ref[0, 0]
    x_ext = jnp.concatenate([xp_ref[0], x_ref[0], xn_ref[0]], axis=0)
    hb_ext = _modulate(x_ext, g1_ref[0], mod[1:2], mod[0:1]).astype(BF16)
    n_blocks = tile // ROW_BLOCK
    blocks = [jnp.dot(perm_ref[...], hb_ext[HALO + blk * ROW_BLOCK:HALO + (blk + 1) * ROW_BLOCK],
                      preferred_element_type=F32).astype(BF16) for blk in range(n_blocks)]
    lhs_s[...] = jnp.concatenate(blocks + [hb_ext[0:HALO], hb_ext[HALO + tile:]], axis=0)
    keep_prev = jnp.where(j > 0, 1.0, 0.0)
    keep_next = jnp.where(j < n_tiles - 1, 1.0, 0.0)
    cw = cw_ref[0]
    cb = cb_ref[0]
    nla_rate, l2a_rate = _decay_rates(lam_ref[0, 0])
    br, bi = br_ref[0, 0], bi_ref[0, 0]
    o_g, o_p, o_t = D_LRU, 2 * D_LRU, 2 * D_LRU + D_POOL
    sub = lax.broadcasted_iota(jnp.int32, (SUBLANES, GATE_GROUP), 0)
    last = ROW_BLOCK - SUBLANES

    def w_cols(lo, width=GATE_GROUP):
        return win_ref[0, :, lo:lo + width]

    def conv(q):
        u = jnp.dot(lhs_s[...], w_cols(q * GATE_GROUP), preferred_element_type=F32)
        ublk = [u[blk * ROW_BLOCK:(blk + 1) * ROW_BLOCK] for blk in range(n_blocks)]
        before = u[tile + HALO - SUBLANES:tile + HALO] * keep_prev
        after = u[tile + HALO:tile + HALO + SUBLANES] * keep_next
        w = [cw[k:k + 1, qcols[q]] for k in range(CONV_WIDTH)]
        out = []
        for blk in range(n_blocks):
            cur = ublk[blk]
            if blk == 0:
                edge1, edge2 = pltpu.roll(before, 1, 0), pltpu.roll(before, 2, 0)
            else:
                edge1 = pltpu.roll(ublk[blk - 1][last:], 1, 0)
                edge2 = pltpu.roll(ublk[blk - 1][last - SUBLANES:last], 1, 0)
            nxt = after if blk == n_blocks - 1 else ublk[blk + 1][0:SUBLANES]
            head1 = jnp.where(sub == 0, edge1, pltpu.roll(cur[last:], 1, 0))
            head2 = jnp.where(sub == 0, edge2, pltpu.roll(cur[last - SUBLANES:last], 1, 0))
            tail = jnp.where(sub == SUBLANES - 1, pltpu.roll(nxt, SUBLANES - 1, 0),
                             pltpu.roll(cur[0:SUBLANES], SUBLANES - 1, 0))
            m1 = jnp.concatenate([head1, cur[:last]], axis=0)
            m2 = jnp.concatenate([head2, m1[:last]], axis=0)
            p1 = jnp.concatenate([cur[SUBLANES:], tail], axis=0)
            out.append(cb[:, qcols[q]] + w[0] * m2 + w[1] * m1 + w[2] * cur + w[3] * p1)
        uc = jnp.concatenate(out, axis=0)
        uc_ref[0, :, qcols[q]] = uc.astype(BF16)
        return uc

    def coeffs(q, uc, blk):
        rws = slice(blk * ROW_BLOCK, (blk + 1) * ROW_BLOCK)
        return _gate_coeffs(uc[rws], wg_ref[0, 0, q], br[:, qcols[q]], bi[:, qcols[q]],
                            nla_rate[:, qcols[q]], l2a_rate[:, qcols[q]])

    def scan(q, ab):
        c = carry_s[:, qcols[q]]
        for blk, (a, b) in enumerate(ab):
            if full:
                def emit(r0, h, base=blk * ROW_BLOCK):
                    hf_ref[0, base + r0:base + r0 + h.shape[0], qcols[q]] = h
            else:
                def emit(r0, h):
                    del r0, h
            c = _scan_block(a, b, c, emit, reverse=False)
        carry_s[:, qcols[q]] = c
        hfin_ref[0, :, qcols[q]] = c

    if not full:
        for q in range(nq):
            uc = conv(q)
            scan(q, [coeffs(q, uc, blk) for blk in range(n_blocks)])
        return

    def pool_mean(gi, p, p_hi, p_lo):
        cols = slice(gi * POOL_GROUP, (gi + 1) * POOL_GROUP)
        wsum = []
        for r0 in range(0, tile, ROW_BLOCK):
            rws = slice(r0, r0 + ROW_BLOCK)
            rhs = jnp.concatenate([p_hi[rws, cols], p_lo[rws, cols]], axis=1)
            s2 = jnp.dot(pm_ref[gi], rhs, preferred_element_type=F32)
            wsum.append(s2[:, :POOL_GROUP] + s2[:, POOL_GROUP:])
        return (jnp.concatenate(wsum, axis=0) * pinv_ref[:, cols] - p[:, cols]).astype(BF16)

    def pool_mix(gi, m):
        return jnp.dot(m, pw_ref[0, gi], preferred_element_type=F32)

    def gelu_cols(q):
        g = jnp.dot(lhs_s[0:tile], w_cols(o_g + q * GATE_GROUP), preferred_element_type=F32)
        gg_ref[0, :, qcols[q]] = _gelu_tanh(g).astype(BF16)

    def lru_gate_cols(q):
        gt = jnp.dot(lhs_s[0:tile], w_cols(o_t + q * GATE_GROUP), preferred_element_type=F32)
        gl_ref[0, :, qcols[q]] = jax.nn.sigmoid(gt).astype(BF16)

    def pool_gate_cols(q, yb):
        gt = jnp.dot(lhs_s[0:tile], w_cols(o_t + D_MODEL + q * GATE_GROUP),
                     preferred_element_type=F32)
        po = jnp.dot(yb, wpo_ref[0, :, qcols[q]], preferred_element_type=F32)
        gp_ref[0, :, qcols[q]] = (jax.nn.sigmoid(gt) * po).astype(BF16)

    ucs = [conv(0), conv(1)]
    p = jnp.dot(lhs_s[0:tile], w_cols(o_p, D_POOL), preferred_element_type=F32)
    p_hi = p.astype(BF16)
    p_lo = (p - p_hi.astype(F32)).astype(BF16)
    ucs.append(conv(2))
    ys = [pool_mix(gi, pool_mean(gi, p, p_hi, p_lo)) for gi in (0, 1)]
    ucs.append(conv(3))
    ab = [coeffs(0, ucs[0], blk) for blk in range(n_blocks)]
    ys += [pool_mix(gi, pool_mean(gi, p, p_hi, p_lo)) for gi in (2, 3)]
    yb = (jnp.concatenate(ys, axis=-1) * ps_ref[0]).astype(BF16)
    for q in range(nq):
        nxt = []
        gelu_cols(q)
        if q + 1 < nq:
            nxt.append(coeffs(q + 1, ucs[q + 1], 0))
        lru_gate_cols(q)
        if q + 1 < nq:
            nxt += [coeffs(q + 1, ucs[q + 1], blk) for blk in range(1, n_blocks)]
        scan(q, ab)
        pool_gate_cols(q, yb)
        ab = nxt


def _fwd_call(x, mods, mod_row, h0, P, l, perm, pm, pinv, *, tile, full):
    bsz, seq, d = x.shape
    n_tiles = seq // tile
    hb = tile // HALO
    n_hb = seq // HALO
    tok = lambda b, j: (b, j, 0)
    tok_spec = pl.BlockSpec((1, tile, d), tok)
    state = pl.BlockSpec((1, 1, D_LRU), lambda b, j: (b, 0, 0))
    args = [x, x, x, mods, P['g1'], P['w_in'], P['cw'], P['cb'], P['wg'], P['br'], P['bi'],
            P['lam'], h0, perm]
    in_specs = [
        tok_spec,
        pl.BlockSpec((1, HALO, d), lambda b, j: (b, jnp.maximum(j * hb - 1, 0), 0)),
        pl.BlockSpec((1, HALO, d), lambda b, j: (b, jnp.minimum((j + 1) * hb, n_hb - 1), 0)),
        pl.BlockSpec((1, 1, N_MOD, d), lambda b, j: (l, mod_row(b), 0, 0)),
        _layer_spec(P['g1'], l), _layer_spec(P['w_in'], l), _layer_spec(P['cw'], l),
        _layer_spec(P['cb'], l), _layer_spec(P['wg'], l, 0), _layer_spec(P['br'], l, 0),
        _layer_spec(P['bi'], l, 0), _layer_spec(P['lam'], l, 0), state, _layer_spec(perm),
    ]
    f32_tok = jax.ShapeDtypeStruct((bsz, seq, D_LRU), F32)
    bf_tok = jax.ShapeDtypeStruct((bsz, seq, D_LRU), BF16)
    st = jax.ShapeDtypeStruct((bsz, 1, D_LRU), F32)
    if full:
        args += [pm, pinv, P['pw'], P['ps'], P['wpo']]
        in_specs += [_layer_spec(pm), _layer_spec(pinv), _layer_spec(P['pw'], l),
                     _layer_spec(P['ps'], l), _layer_spec(P['wpo'], l)]
        out_specs = [tok_spec, tok_spec, state, tok_spec, tok_spec, tok_spec]
        out_shape = [bf_tok, f32_tok, st, bf_tok, bf_tok, bf_tok]
    else:
        out_specs = [tok_spec, state]
        out_shape = [bf_tok, st]
    return pl.pallas_call(
        functools.partial(_fwd_kernel, tile=tile, n_tiles=n_tiles, full=full),
        grid=(bsz, n_tiles),
        in_specs=in_specs,
        out_specs=out_specs,
        out_shape=out_shape,
        scratch_shapes=[pltpu.VMEM((tile + 2 * HALO, d), BF16), pltpu.VMEM((1, D_LRU), F32)],
        compiler_params=pltpu.CompilerParams(
            dimension_semantics=("arbitrary", "arbitrary"), vmem_limit_bytes=VMEM_LIMIT),
        name=f"fwd_t{tile}" if full else f"fwdstate_t{tile}",
    )(*args)


def _bwd_kernel(*refs, tile, full):
    if full:
        (uc_ref, wg_ref, br_ref, bi_ref, lam_ref, h0_ref, x_ref, hf_ref, gg_ref, gl_ref, gp_ref,
         mod_ref, wlo_ref, wo_ref, unperm_ref, hfin_ref, out_ref, hl_s, carry_s) = refs
    else:
        (uc_ref, wg_ref, br_ref, bi_ref, lam_ref, h0_ref, hfin_ref, carry_s) = refs
    j = pl.program_id(1)
    nq = N_GATE_GROUPS
    qcols = _group_cols()

    @pl.when(j == 0)
    def _():
        carry_s[...] = h0_ref[0]

    nla_rate, l2a_rate = _decay_rates(lam_ref[0, 0])
    br, bi = br_ref[0, 0], bi_ref[0, 0]

    def coeffs(q):
        return _gate_coeffs(uc_ref[0, :, qcols[q]], wg_ref[0, 0, q], br[:, qcols[q]],
                            bi[:, qcols[q]], nla_rate[:, qcols[q]], l2a_rate[:, qcols[q]])

    def scan(q, ab):
        if full:
            def emit(r0, h):
                rws = slice(r0, r0 + h.shape[0])
                hl_s[rws, qcols[q]] = h + hf_ref[0, rws, qcols[q]]
        else:
            def emit(r0, h):
                del r0, h
        c = _scan_blocks(ab[0], ab[1], carry_s[:, qcols[q]], emit, reverse=True)
        carry_s[:, qcols[q]] = c
        hfin_ref[0, :, qcols[q]] = c

    ab = coeffs(0)
    lru_out = None
    for q in range(nq):
        nxt = coeffs(q + 1) if q + 1 < nq else None
        scan(q, ab)
        ab = nxt
        if full:
            lru_in = (hl_s[:, qcols[q]] * gg_ref[0, :, qcols[q]].astype(F32)).astype(BF16)
            part = jnp.dot(lru_in, wlo_ref[0, qcols[q], :], preferred_element_type=F32)
            lru_out = part if lru_out is None else lru_out + part
    if not full:
        return

    merged = gl_ref[0].astype(F32) * lru_out + gp_ref[0].astype(F32)
    yo = jnp.dot(_to_token_order(merged.astype(BF16), unperm_ref), wo_ref[0],
                 preferred_element_type=F32)
    out_ref[0] = x_ref[0] + mod_ref[0, 0][2:3] * yo


def _bwd_call(uc, h0, P, l, x=None, hf=None, gg=None, gl=None, gp=None, mods=None, mod_row=None,
              unperm=None, *, tile, full):
    bsz, seq, d = uc.shape
    n_tiles = seq // tile
    tok = lambda b, j: (b, n_tiles - 1 - j, 0)
    tok_spec = pl.BlockSpec((1, tile, d), tok)
    state = pl.BlockSpec((1, 1, D_LRU), lambda b, j: (b, 0, 0))
    args = [uc, P['wg'], P['br'], P['bi'], P['lam'], h0]
    in_specs = [tok_spec, _layer_spec(P['wg'], l, 1), _layer_spec(P['br'], l, 1),
                _layer_spec(P['bi'], l, 1), _layer_spec(P['lam'], l, 1), state]
    st = jax.ShapeDtypeStruct((bsz, 1, D_LRU), F32)
    scratch = [pltpu.VMEM((1, D_LRU), F32)]
    if full:
        args += [x, hf, gg, gl, gp, mods, P['wlo'], P['wo'], unperm]
        in_specs += [tok_spec] * 5 + [
            pl.BlockSpec((1, 1, N_MOD, d), lambda b, j: (l, mod_row(b), 0, 0)),
            _layer_spec(P['wlo'], l), _layer_spec(P['wo'], l), _layer_spec(unperm)]
        out_specs = [state, tok_spec]
        out_shape = [st, jax.ShapeDtypeStruct((bsz, seq, d), F32)]
        scratch = [pltpu.VMEM((tile, D_LRU), F32)] + scratch
    else:
        out_specs = state
        out_shape = st
    return pl.pallas_call(
        functools.partial(_bwd_kernel, tile=tile, full=full),
        grid=(bsz, n_tiles),
        in_specs=in_specs,
        out_specs=out_specs,
        out_shape=out_shape,
        scratch_shapes=scratch,
        compiler_params=pltpu.CompilerParams(
            dimension_semantics=("arbitrary", "arbitrary"), vmem_limit_bytes=VMEM_LIMIT),
        name=f"bwd_t{tile}" if full else f"bwdstate_t{tile}",
    )(*args)


def _mlp_kernel(x_ref, mod_ref, g2_ref, w1_ref, w2_ref, fg_ref, out_ref, hm_s, acc_s, *,
                n_chunks, final_norm):
    k = pl.program_id(2)

    @pl.when(k == 0)
    def _():
        mod = mod_ref[0, 0]
        hm_s[...] = _modulate(x_ref[0], g2_ref[0], mod[4:5], mod[3:4]).astype(BF16)
        acc_s[...] = jnp.zeros_like(acc_s)

    hid = jnp.dot(hm_s[...], w1_ref[0], preferred_element_type=F32)
    hid = jnp.square(jnp.maximum(hid, 0.0))
    acc_s[...] += jnp.dot(hid.astype(BF16), w2_ref[0], preferred_element_type=F32)

    @pl.when(k == n_chunks - 1)
    def _():
        y = x_ref[0] + mod_ref[0, 0][5:6] * acc_s[...]
        if final_norm:
            ms = jnp.mean(y * y, axis=-1, keepdims=True)
            y = y * lax.rsqrt(ms + EPS) * fg_ref[...]
        out_ref[0] = y


def _mlp_call(x, mods, mod_row, P, l, fg, *, tile, final_norm):
    bsz, seq, d = x.shape
    n_chunks = D_FF // FF_CHUNK
    tok = pl.BlockSpec((1, tile, d), lambda b, j, k: (b, j, 0))
    return pl.pallas_call(
        functools.partial(_mlp_kernel, n_chunks=n_chunks, final_norm=final_norm),
        grid=(bsz, seq // tile, n_chunks),
        in_specs=[
            tok,
            pl.BlockSpec((1, 1, N_MOD, d), lambda b, j, k: (l, mod_row(b), 0, 0)),
            _layer_spec(P['g2'], l),
            pl.BlockSpec((1, d, FF_CHUNK), lambda b, j, k: (l, 0, k)),
            pl.BlockSpec((1, FF_CHUNK, d), lambda b, j, k: (l, k, 0)),
            _layer_spec(fg),
        ],
        out_specs=tok,
        out_shape=jax.ShapeDtypeStruct((bsz, seq, d), F32),
        scratch_shapes=[pltpu.VMEM((tile, d), BF16), pltpu.VMEM((tile, d), F32)],
        compiler_params=pltpu.CompilerParams(
            dimension_semantics=("arbitrary", "arbitrary", "arbitrary"),
            vmem_limit_bytes=VMEM_LIMIT),
        name=f"mlp_t{tile}",
    )(x, mods, P['g2'], P['w1'], P['w2'], fg)


def _bwd_mlp_kernel(uc_ref, hf_ref, gg_ref, gl_ref, gp_ref, x_ref, wg_ref, br_ref, bi_ref, lam_ref,
                    h0_ref, mod_mix_ref, mod_mlp_ref, wlo_ref, wo_ref, unperm_ref, g2_ref, w1_ref,
                    w2_ref, fg_ref, out_ref, lru_s, x1_s, hm_s, hid_s, carry_s, *, tile, n_tiles,
                    final_norm):
    s = pl.program_id(0)
    n_steps = pl.num_programs(0) - 1

    @pl.when(s % n_tiles == 0)
    def _():
        carry_s[...] = h0_ref[0]

    args = (uc_ref, hf_ref, gg_ref, gl_ref, gp_ref, x_ref, wg_ref, br_ref, bi_ref, lam_ref,
            mod_mix_ref, mod_mlp_ref, wlo_ref, wo_ref, unperm_ref, g2_ref, w1_ref, w2_ref, fg_ref,
            out_ref, lru_s, x1_s, hm_s, hid_s, carry_s)
    step = functools.partial(_bwd_mlp_step, *args, tile=tile, final_norm=final_norm)
    pl.when(s == 0)(functools.partial(step, mixer=True, mlp=False))
    pl.when((s > 0) & (s < n_steps))(functools.partial(step, mixer=True, mlp=True))
    pl.when(s == n_steps)(functools.partial(step, mixer=False, mlp=True))


def _bwd_mlp_step(uc_ref, hf_ref, gg_ref, gl_ref, gp_ref, x_ref, wg_ref, br_ref, bi_ref, lam_ref,
                  mod_mix_ref, mod_mlp_ref, wlo_ref, wo_ref, unperm_ref, g2_ref, w1_ref, w2_ref,
                  fg_ref, out_ref, lru_s, x1_s, hm_s, hid_s, carry_s, *, tile, final_norm, mixer,
                  mlp):
    nq = N_GATE_GROUPS
    qcols = _group_cols()

    nla_rate, l2a_rate = _decay_rates(lam_ref[0, 0])
    br, bi = br_ref[0, 0], bi_ref[0, 0]
    state = {}

    half = ROW_BLOCK
    assert tile == 2 * half

    def coeffs(q, h):
        rws = slice(h * half, (h + 1) * half)
        state['ab', q, h] = _gate_coeffs(
            uc_ref[0, rws, qcols[q]], wg_ref[0, 0, q], br[:, qcols[q]], bi[:, qcols[q]],
            nla_rate[:, qcols[q]], l2a_rate[:, qcols[q]])

    def scan(q, h):
        def emit(r0, hrows):
            rws = slice(h * half + r0, h * half + r0 + hrows.shape[0])
            gated = (hrows + hf_ref[0, rws, qcols[q]]) * gg_ref[0, rws, qcols[q]].astype(F32)
            lru_s[rws, qcols[q]] = gated.astype(BF16)
        a, b = state.pop(('ab', q, h))
        carry_s[:, qcols[q]] = _scan_block(a, b, carry_s[:, qcols[q]], emit, reverse=True)

    def merge():
        lru_out = jnp.dot(lru_s[...], wlo_ref[0], preferred_element_type=F32)
        merged = gl_ref[0].astype(F32) * lru_out + gp_ref[0].astype(F32)
        state['merged'] = _to_token_order(merged.astype(BF16), unperm_ref)

    def project():
        yo = jnp.dot(state.pop('merged'), wo_ref[0], preferred_element_type=F32)
        state['x1'] = x_ref[0] + mod_mix_ref[0, 0][2:3] * yo

    mod = mod_mlp_ref[0, 0]

    def mlp_in():
        hm_s[...] = _modulate(x1_s[...], g2_ref[0], mod[4:5], mod[3:4]).astype(BF16)

    def mlp_up(k):
        cols = slice(k * MLP_FUSED_CHUNK, (k + 1) * MLP_FUSED_CHUNK)
        hid = jnp.dot(hm_s[...], w1_ref[0, :, cols], preferred_element_type=F32)
        hid_s[:, cols] = jnp.square(jnp.maximum(hid, 0.0)).astype(BF16)

    def mlp_down(n):
        cols = qcols[n]
        res = jnp.dot(hid_s[...], w2_ref[0, :, cols], preferred_element_type=F32)
        y = x1_s[:, cols] + mod[5:6, cols] * res
        out_ref[0, :, cols] = y
        if final_norm:
            part = jnp.sum(y * y, axis=-1, keepdims=True)
            state['ssq'] = part if n == 0 else state['ssq'] + part

    def mlp_out():
        if final_norm:
            scale = lax.rsqrt(state.pop('ssq') * (1.0 / D_MODEL) + EPS)
            out_ref[0] = out_ref[0] * scale * fg_ref[...]

    mxu_items = ([functools.partial(mlp_up, k) for k in range(D_FF // MLP_FUSED_CHUNK)]
                 + [functools.partial(mlp_down, n) for n in range(nq)])
    vec_items = [lambda: coeffs(0, 1), lambda: coeffs(0, 0)]
    for q in range(nq):
        vec_items.append(functools.partial(scan, q, 1))
        if q + 1 < nq:
            vec_items.append(functools.partial(coeffs, q + 1, 1))
        vec_items.append(functools.partial(scan, q, 0))
        if q + 1 < nq:
            vec_items.append(functools.partial(coeffs, q + 1, 0))
    ahead = [[], [], [], [], [], [], [merge], [project]]
    spread = (3, 3, 3, 3, 2, 2)
    assert sum(spread) == len(vec_items) and len(ahead) == len(mxu_items)
    it = iter(vec_items)
    for i, n_vec in enumerate(spread):
        ahead[i] = [next(it) for _ in range(n_vec)]
    if mixer and mlp:
        mlp_in()
        for vecs, mxu in zip(ahead, mxu_items):
            for vec in vecs:
                vec()
            mxu()
        mlp_out()
    elif mlp:
        mlp_in()
        for mxu in mxu_items:
            mxu()
        mlp_out()
    else:
        for vecs in ahead:
            for vec in vecs:
                vec()
    if mixer:
        x1_s[...] = state.pop('x1')


def _bwd_mlp_call(uc, hf, gg, gl, gp, x, h0, mods, P, l, unperm, fg, *, tile, final_norm):
    bsz, seq, d = x.shape
    n_tiles = seq // tile
    n_steps = bsz * n_tiles

    def tile_of(s):
        s = jnp.clip(s, 0, n_steps - 1)
        return s // n_tiles, n_tiles - 1 - s % n_tiles

    def tok(shift):
        def index_map(s):
            b, j = tile_of(s - shift)
            return (b, j, 0)
        return pl.BlockSpec((1, tile, d), index_map)

    def mod_spec(shift):
        return pl.BlockSpec((1, 1, N_MOD, d), lambda s: (l, tile_of(s - shift)[0], 0, 0))

    args = [uc, hf, gg, gl, gp, x, P['wg'], P['br'], P['bi'], P['lam'], h0, mods, mods,
            P['wlo'], P['wo'], unperm, P['g2'], P['w1'], P['w2'], fg]
    in_specs = [tok(0)] * 6 + [
        _layer_spec(P['wg'], l, 1), _layer_spec(P['br'], l, 1), _layer_spec(P['bi'], l, 1),
        _layer_spec(P['lam'], l, 1),
        pl.BlockSpec((1, 1, D_LRU), lambda s: (tile_of(s)[0], 0, 0)),
        mod_spec(0), mod_spec(1),
        _layer_spec(P['wlo'], l), _layer_spec(P['wo'], l), _layer_spec(unperm),
        _layer_spec(P['g2'], l), _layer_spec(P['w1'], l), _layer_spec(P['w2'], l), _layer_spec(fg)]
    return pl.pallas_call(
        functools.partial(_bwd_mlp_kernel, tile=tile, n_tiles=n_tiles, final_norm=final_norm),
        grid=(n_steps + 1,),
        in_specs=in_specs,
        out_specs=tok(1),
        out_shape=jax.ShapeDtypeStruct((bsz, seq, d), F32),
        scratch_shapes=[
            pltpu.VMEM((tile, D_LRU), BF16),
            pltpu.VMEM((tile, d), F32),
            pltpu.VMEM((tile, d), BF16),
            pltpu.VMEM((tile, D_FF), BF16),
            pltpu.VMEM((1, D_LRU), F32),
        ],
        compiler_params=pltpu.CompilerParams(
            dimension_semantics=("arbitrary",), vmem_limit_bytes=VMEM_LIMIT),
        name=f"bwdmlp_t{tile}",
    )(*args)


def _interleave_tables():
    r = np.arange(ROW_BLOCK)
    token = N_STEPS * (r % SUBLANES) + r // SUBLANES
    perm = np.zeros((ROW_BLOCK, ROW_BLOCK), np.float32)
    perm[r, token] = 1.0
    return token, jnp.asarray(perm, BF16), jnp.asarray(perm.T, BF16)


def _pool_tables(tile, row_len, token):
    assert ROW_BLOCK % row_len == 0 and tile % ROW_BLOCK == 0
    pos = token % row_len
    same_row = (token[:, None] // row_len) == (token[None, :] // row_len)
    mats, invs = [], []
    for w in POOL_WINDOWS:
        lo = np.clip(pos - w // 2, 0, row_len)
        hi = np.clip(pos + w - w // 2, 0, row_len)
        member = same_row & (pos[None, :] >= lo[:, None]) & (pos[None, :] < hi[:, None])
        mats.append(member.astype(np.float32))
        inv = np.repeat((1.0 / (hi - lo).astype(np.float32))[:, None], POOL_GROUP, axis=1)
        invs.append(np.tile(inv, (tile // ROW_BLOCK, 1)))
    return jnp.asarray(np.stack(mats), BF16), jnp.asarray(np.concatenate(invs, axis=1), F32)


def _gate_weights(w_r, w_i):
    hpg = GATE_GROUP // LRU_HEAD_DIM
    eye = jnp.eye(hpg, dtype=BF16)

    def blockdiag(w):
        w6 = w.astype(BF16).reshape(w.shape[0], 2, N_GATE_GROUPS, hpg, LRU_HEAD_DIM, LRU_HEAD_DIM)
        bd = jnp.einsum('ldqhie,hk->ldqhike', w6, eye)
        return bd.reshape(w.shape[0], 2, N_GATE_GROUPS, GATE_GROUP, GATE_GROUP)

    return jnp.concatenate([blockdiag(w_r), blockdiag(w_i)], axis=-1)


def kernel(x, c, ctx, c_ctx, w_ada, b_ada, norm1_g, norm2_g, w_in, conv_w, conv_b, lru_w_r, lru_b_r,
           lru_w_i, lru_b_i, lru_lambda, w_lru_out, pool_w, pool_scale, w_pool_out, w_o, mlp_w1,
           mlp_w2, final_g):
    bsz, seq, d = x.shape
    ctx_len = ctx.shape[1]
    depth = w_ada.shape[0]
    assert d == D_MODEL and seq % SEQ_TILE == 0
    assert bsz + 1 <= SUBLANES and ctx_len % HALO == 0

    cc = jnp.concatenate([c, c_ctx[None], jnp.zeros((SUBLANES - bsz - 1, d), F32)], axis=0)
    mods = _ada_call(cc, w_ada, b_ada).reshape(depth, SUBLANES, N_MOD, d)
    lat_row = lambda b: b
    ctx_row = lambda b: bsz

    token, perm, unperm = _interleave_tables()
    pm_lat, pinv_lat = _pool_tables(SEQ_TILE, GRID_W, token)
    pm_ctx, pinv_ctx = _pool_tables(ctx_len, ctx_len, token)
    zeros_state = jnp.zeros((bsz, 1, D_LRU), F32)
    fg = final_g.reshape(1, d)
    rows = lambda v: v.reshape(v.shape[:-1] + (1, v.shape[-1]))
    P = dict(
        g1=rows(norm1_g), g2=rows(norm2_g), w_in=w_in.astype(BF16), cw=conv_w, cb=rows(conv_b),
        wg=_gate_weights(lru_w_r, lru_w_i), br=rows(lru_b_r), bi=rows(lru_b_i),
        lam=rows(lru_lambda), pw=pool_w.astype(BF16), ps=rows(pool_scale),
        wpo=w_pool_out.astype(BF16), wlo=w_lru_out.astype(BF16), wo=w_o.astype(BF16),
        w1=mlp_w1.astype(BF16), w2=mlp_w2.astype(BF16))

    for l in range(depth):
        last = l == depth - 1
        if last:
            uc_c, hfin_f = _fwd_call(ctx, mods, ctx_row, zeros_state, P, l, perm, None, None,
                                     tile=ctx_len, full=False)
            hfin_b = _bwd_call(uc_c, zeros_state, P, l, tile=ctx_len, full=False)
        else:
            uc_c, hf_c, hfin_f, gg_c, gl_c, gp_c = _fwd_call(
                ctx, mods, ctx_row, zeros_state, P, l, perm, pm_ctx, pinv_ctx, tile=ctx_len,
                full=True)
            hfin_b, ctx = _bwd_call(uc_c, zeros_state, P, l, ctx, hf_c, gg_c, gl_c, gp_c, mods,
                                    ctx_row, unperm, tile=ctx_len, full=True)
            ctx = _mlp_call(ctx.reshape(1, bsz * ctx_len, d), mods, ctx_row, P, l, fg,
                            tile=bsz * ctx_len, final_norm=False).reshape(bsz, ctx_len, d)

        uc_l, hf_l, _, gg_l, gl_l, gp_l = _fwd_call(
            x, mods, lat_row, hfin_f, P, l, perm, pm_lat, pinv_lat, tile=SEQ_TILE, full=True)
        x = _bwd_mlp_call(uc_l, hf_l, gg_l, gl_l, gp_l, x, hfin_b, mods, P, l, unperm, fg,
                          tile=SEQ_TILE, final_norm=last)
    return x
```

```python
import functools

import jax
import jax.numpy as jnp
import numpy as np
from jax import lax
from jax.experimental import pallas as pl
from jax.experimental.pallas import tpu as pltpu

D_MODEL = 1024
D_LRU = 1024
LRU_HEADS = 16
LRU_HEAD_DIM = D_LRU // LRU_HEADS
CONV_WIDTH = 4
CONV_LEFT = 2
LRU_C = 8.0
D_POOL = 512
POOL_WINDOWS = (2, 4, 8, 16)
POOL_GROUP = D_POOL // len(POOL_WINDOWS)
D_FF = 4 * D_MODEL
N_MOD = 6
GRID_W = 64
EPS = 1e-6
LOG2_E = 1.4426950408889634

SUBLANES = 8
HALO = 16
GATE_GROUP = 256
N_GATE_GROUPS = D_LRU // GATE_GROUP
ROW_BLOCK = 256
N_STEPS = ROW_BLOCK // SUBLANES
SCAN_SEGMENT = 8
SEQ_TILE = 512
FF_CHUNK = 1024
MLP_FUSED_CHUNK = 1024
ADA_CHUNK = 1536
VMEM_LIMIT = 56 * 1024 * 1024

F32 = jnp.float32
BF16 = jnp.bfloat16


def _layer_spec(arr, *lead):
    n_lead = len(lead)
    shape = (1,) * n_lead + arr.shape[n_lead:]
    idx = tuple(lead) + (0,) * (arr.ndim - n_lead)
    return pl.BlockSpec(shape, lambda *_: idx, pipeline_mode=pl.Buffered(1))


def _modulate(x, g, scale, shift):
    ms = jnp.mean(x * x, axis=-1, keepdims=True)
    return (x * lax.rsqrt(ms + EPS) * g) * (1.0 + scale) + shift


def _softplus(x):
    return jnp.maximum(x, 0.0) + jnp.log1p(jnp.exp(-jnp.abs(x)))


GELU_C1 = 0.7978845608028654
GELU_C2 = 0.044715 * GELU_C1


def _gelu_tanh(x):
    hx = 0.5 * x
    return hx + hx * jnp.tanh(x * (GELU_C1 + GELU_C2 * (x * x)))


def _ada_kernel(c_ref, w_ref, b_ref, o_ref):
    c = c_ref[...]
    s = c * jax.nn.sigmoid(c)
    o_ref[0] = jnp.dot(s, w_ref[0], preferred_element_type=F32,
                       precision=lax.Precision.HIGHEST) + b_ref[0]


def _ada_call(cc, w_ada, b_ada):
    depth, d, n = w_ada.shape
    return pl.pallas_call(
        _ada_kernel,
        grid=(depth, n // ADA_CHUNK),
        in_specs=[
            pl.BlockSpec((SUBLANES, d), lambda l, k: (0, 0)),
            pl.BlockSpec((1, d, ADA_CHUNK), lambda l, k: (l, 0, k)),
            pl.BlockSpec((1, 1, ADA_CHUNK), lambda l, k: (l, 0, k)),
        ],
        out_specs=pl.BlockSpec((1, SUBLANES, ADA_CHUNK), lambda l, k: (l, 0, k)),
        out_shape=jax.ShapeDtypeStruct((depth, SUBLANES, n), F32),
        compiler_params=pltpu.CompilerParams(
            dimension_semantics=("arbitrary", "arbitrary"), vmem_limit_bytes=VMEM_LIMIT),
        name="adaln",
    )(cc, w_ada, b_ada.reshape(depth, 1, n))


def _decay_rates(lam):
    sp = _softplus(-lam)
    return LRU_C * sp, (-LRU_C * LOG2_E) * sp


def _gate_coeffs(uc, wg, br, bi, nla_rate, l2a_rate):
    z = jnp.dot(uc.astype(BF16), wg, preferred_element_type=F32)
    r = jax.nn.sigmoid(z[:, :GATE_GROUP] + br)
    i = jax.nn.sigmoid(z[:, GATE_GROUP:] + bi)
    a = jnp.exp2(r * l2a_rate)
    t = jnp.tanh(r * nla_rate) * (1.0 + a * a)
    mult = jnp.where(t > 0.0, t * lax.rsqrt(t), 0.0)
    return a, mult * (i * uc.astype(F32))


def _group_scan(ag, bg, c, reverse):
    sub = lax.broadcasted_iota(jnp.int32, ag.shape, 0)
    for k in (1, 2, 4):
        shift = (SUBLANES - k) if reverse else k
        m = (sub < SUBLANES - k) if reverse else (sub >= k)
        ra = pltpu.roll(ag, shift, 0)
        rb = pltpu.roll(bg, shift, 0)
        bg = jnp.where(m, ag * rb, 0.0) + bg
        ag = jnp.where(m, ag * ra, ag)
    return ag * c + bg


def _scan_block(a, b, c, emit, reverse):
    sub = lax.broadcasted_iota(jnp.int32, (SUBLANES, a.shape[1]), 0)
    steps = list(range(N_STEPS - 1, -1, -1) if reverse else range(N_STEPS))
    rows_of = lambda k: slice(k * SUBLANES, (k + 1) * SUBLANES)
    h = acc = None
    kept = {}
    for i, k in enumerate(steps):
        h = b[rows_of(k)] if h is None else a[rows_of(k)] * h + b[rows_of(k)]
        acc = a[rows_of(k)] if acc is None else a[rows_of(k)] * acc
        if (i + 1) % SCAN_SEGMENT == 0 and i + 1 < N_STEPS:
            kept[i + 1] = (h, acc)
    ends = _group_scan(acc, h, c, reverse)
    if reverse:
        start = jnp.where(sub == SUBLANES - 1, c, pltpu.roll(ends, SUBLANES - 1, 0))
        c = ends[0:1]
    else:
        start = jnp.where(sub == 0, c, pltpu.roll(ends, 1, 0))
        c = ends[SUBLANES - 1:SUBLANES]
    done = {}
    for first in range(0, N_STEPS, SCAN_SEGMENT):
        h = start if first == 0 else kept[first][0] + kept[first][1] * start
        for k in steps[first:first + SCAN_SEGMENT]:
            h = a[rows_of(k)] * h + b[rows_of(k)]
            done[k] = h
            pair = k - k % 2
            if pair in done and pair + 1 in done:
                emit(pair * SUBLANES, jnp.concatenate([done.pop(pair), done.pop(pair + 1)], axis=0))
    return c


def _scan_blocks(a, b, c, emit, reverse):
    n_blocks = a.shape[0] // ROW_BLOCK
    for blk in (range(n_blocks - 1, -1, -1) if reverse else range(n_blocks)):
        base = blk * ROW_BLOCK
        rws = slice(base, base + ROW_BLOCK)
        c = _scan_block(a[rws], b[rws], c, lambda r0, h, base=base: emit(base + r0, h), reverse)
    return c


def _group_cols():
    return [slice(q * GATE_GROUP, (q + 1) * GATE_GROUP) for q in range(N_GATE_GROUPS)]


def _to_token_order(v, unperm_ref):
    blocks = [jnp.dot(unperm_ref[...], v[r0:r0 + ROW_BLOCK], preferred_element_type=F32)
              for r0 in range(0, v.shape[0], ROW_BLOCK)]
    return jnp.concatenate(blocks, axis=0).astype(BF16)


def _fwd_kernel(*refs, tile, n_tiles, full):
    if full:
        (x_ref, xp_ref, xn_ref, mod_ref, g1_ref, win_ref, cw_ref, cb_ref, wg_ref, br_ref, bi_ref,
         lam_ref, h0_ref, perm_ref, pm_ref, pinv_ref, pw_ref, ps_ref, wpo_ref,
         uc_ref, hf_ref, hfin_ref, gg_ref, gl_ref, gp_ref, lhs_s, carry_s) = refs
    else:
        (x_ref, xp_ref, xn_ref, mod_ref, g1_ref, win_ref, cw_ref, cb_ref, wg_ref, br_ref, bi_ref,
         lam_ref, h0_ref, perm_ref, uc_ref, hfin_ref, lhs_s, carry_s) = refs
    j = pl.program_id(1)
    nq = N_GATE_GROUPS
    qcols = _group_cols()

    @pl.when(j == 0)
    def _():
        carry_s[...] = h0_ref[0]

    mod = mod_ref[0, 0]
    x_ext = jnp.concatenate([xp_ref[0], x_ref[0], xn_ref[0]], axis=0)
    hb_ext = _modulate(x_ext, g1_ref[0], mod[1:2], mod[0:1]).astype(BF16)
    n_blocks = tile // ROW_BLOCK
    blocks = [jnp.dot(perm_ref[...], hb_ext[HALO + blk * ROW_BLOCK:HALO + (blk + 1) * ROW_BLOCK],
                      preferred_element_type=F32).astype(BF16) for blk in range(n_blocks)]
    lhs_s[...] = jnp.concatenate(blocks + [hb_ext[0:HALO], hb_ext[HALO + tile:]], axis=0)
    keep_prev = jnp.where(j > 0, 1.0, 0.0)
    keep_next = jnp.where(j < n_tiles - 1, 1.0, 0.0)
    cw = cw_ref[0]
    cb = cb_ref[0]
    nla_rate, l2a_rate = _decay_rates(lam_ref[0, 0])
    br, bi = br_ref[0, 0], bi_ref[0, 0]
    o_g, o_p, o_t = D_LRU, 2 * D_LRU, 2 * D_LRU + D_POOL
    sub = lax.broadcasted_iota(jnp.int32, (SUBLANES, GATE_GROUP), 0)
    last = ROW_BLOCK - SUBLANES

    def w_cols(lo, width=GATE_GROUP):
        return win_ref[0, :, lo:lo + width]

    def conv(q):
        u = jnp.dot(lhs_s[...], w_cols(q * GATE_GROUP), preferred_element_type=F32)
        ublk = [u[blk * ROW_BLOCK:(blk + 1) * ROW_BLOCK] for blk in range(n_blocks)]
        before = u[tile + HALO - SUBLANES:tile + HALO] * keep_prev
        after = u[tile + HALO:tile + HALO + SUBLANES] * keep_next
        w = [cw[k:k + 1, qcols[q]] for k in range(CONV_WIDTH)]
        out = []
        for blk in range(n_blocks):
            cur = ublk[blk]
            if blk == 0:
                edge1, edge2 = pltpu.roll(before, 1, 0), pltpu.roll(before, 2, 0)
            else:
                edge1 = pltpu.roll(ublk[blk - 1][last:], 1, 0)
                edge2 = pltpu.roll(ublk[blk - 1][last - SUBLANES:last], 1, 0)
            nxt = after if blk == n_blocks - 1 else ublk[blk + 1][0:SUBLANES]
            head1 = jnp.where(sub == 0, edge1, pltpu.roll(cur[last:], 1, 0))
            head2 = jnp.where(sub == 0, edge2, pltpu.roll(cur[last - SUBLANES:last], 1, 0))
            tail = jnp.where(sub == SUBLANES - 1, pltpu.roll(nxt, SUBLANES - 1, 0),
                             pltpu.roll(cur[0:SUBLANES], SUBLANES - 1, 0))
            m1 = jnp.concatenate([head1, cur[:last]], axis=0)
            m2 = jnp.concatenate([head2, m1[:last]], axis=0)
            p1 = jnp.concatenate([cur[SUBLANES:], tail], axis=0)
            out.append(cb[:, qcols[q]] + w[0] * m2 + w[1] * m1 + w[2] * cur + w[3] * p1)
        uc_ref[0, :, qcols[q]] = jnp.concatenate(out, axis=0).astype(BF16)

    def coeffs(q, blk):
        rws = slice(blk * ROW_BLOCK, (blk + 1) * ROW_BLOCK)
        return _gate_coeffs(uc_ref[0, rws, qcols[q]], wg_ref[0, 0, q], br[:, qcols[q]],
                            bi[:, qcols[q]], nla_rate[:, qcols[q]], l2a_rate[:, qcols[q]])

    def scan(q, ab):
        c = carry_s[:, qcols[q]]
        for blk, (a, b) in enumerate(ab):
            if full:
                def emit(r0, h, base=blk * ROW_BLOCK):
                    hf_ref[0, base + r0:base + r0 + h.shape[0], qcols[q]] = h
            else:
                def emit(r0, h):
                    del r0, h
            c = _scan_block(a, b, c, emit, reverse=False)
        carry_s[:, qcols[q]] = c
        hfin_ref[0, :, qcols[q]] = c

    if not full:
        for q in range(nq):
            conv(q)
            scan(q, [coeffs(q, blk) for blk in range(n_blocks)])
        return

    def pool_mean(gi, p, p_hi, p_lo):
        cols = slice(gi * POOL_GROUP, (gi + 1) * POOL_GROUP)
        wsum = []
        for r0 in range(0, tile, ROW_BLOCK):
            rws = slice(r0, r0 + ROW_BLOCK)
            rhs = jnp.concatenate([p_hi[rws, cols], p_lo[rws, cols]], axis=1)
            s2 = jnp.dot(pm_ref[gi], rhs, preferred_element_type=F32)
            wsum.append(s2[:, :POOL_GROUP] + s2[:, POOL_GROUP:])
        return (jnp.concatenate(wsum, axis=0) * pinv_ref[:, cols] - p[:, cols]).astype(BF16)

    def pool_mix(gi, m):
        return jnp.dot(m, pw_ref[0, gi], preferred_element_type=F32)

    def gelu_cols(q):
        g = jnp.dot(lhs_s[0:tile], w_cols(o_g + q * GATE_GROUP), preferred_element_type=F32)
        gg_ref[0, :, qcols[q]] = _gelu_tanh(g).astype(BF16)

    def lru_gate_cols(q):
        gt = jnp.dot(lhs_s[0:tile], w_cols(o_t + q * GATE_GROUP), preferred_element_type=F32)
        gl_ref[0, :, qcols[q]] = jax.nn.sigmoid(gt).astype(BF16)

    def pool_gate_cols(q, yb):
        gt = jnp.dot(lhs_s[0:tile], w_cols(o_t + D_MODEL + q * GATE_GROUP),
                     preferred_element_type=F32)
        po = jnp.dot(yb, wpo_ref[0, :, qcols[q]], preferred_element_type=F32)
        gp_ref[0, :, qcols[q]] = (jax.nn.sigmoid(gt) * po).astype(BF16)

    conv(0)
    conv(1)
    p = jnp.dot(lhs_s[0:tile], w_cols(o_p, D_POOL), preferred_element_type=F32)
    p_hi = p.astype(BF16)
    p_lo = (p - p_hi.astype(F32)).astype(BF16)
    conv(2)
    ys = [pool_mix(gi, pool_mean(gi, p, p_hi, p_lo)) for gi in (0, 1)]
    conv(3)
    ab = [coeffs(0, blk) for blk in range(n_blocks)]
    ys += [pool_mix(gi, pool_mean(gi, p, p_hi, p_lo)) for gi in (2, 3)]
    yb = (jnp.concatenate(ys, axis=-1) * ps_ref[0]).astype(BF16)
    for q in range(nq):
        nxt = []
        gelu_cols(q)
        if q + 1 < nq:
            nxt.append(coeffs(q + 1, 0))
        lru_gate_cols(q)
        if q + 1 < nq:
            nxt += [coeffs(q + 1, blk) for blk in range(1, n_blocks)]
        scan(q, ab)
        pool_gate_cols(q, yb)
        ab = nxt


def _fwd_call(x, mods, mod_row, h0, P, l, perm, pm, pinv, *, tile, full):
    bsz, seq, d = x.shape
    n_tiles = seq // tile
    hb = tile // HALO
    n_hb = seq // HALO
    tok = lambda b, j: (b, j, 0)
    tok_spec = pl.BlockSpec((1, tile, d), tok)
    state = pl.BlockSpec((1, 1, D_LRU), lambda b, j: (b, 0, 0))
    args = [x, x, x, mods, P['g1'], P['w_in'], P['cw'], P['cb'], P['wg'], P['br'], P['bi'],
            P['lam'], h0, perm]
    in_specs = [
        tok_spec,
        pl.BlockSpec((1, HALO, d), lambda b, j: (b, jnp.maximum(j * hb - 1, 0), 0)),
        pl.BlockSpec((1, HALO, d), lambda b, j: (b, jnp.minimum((j + 1) * hb, n_hb - 1), 0)),
        pl.BlockSpec((1, 1, N_MOD, d), lambda b, j: (l, mod_row(b), 0, 0)),
        _layer_spec(P['g1'], l), _layer_spec(P['w_in'], l), _layer_spec(P['cw'], l),
        _layer_spec(P['cb'], l), _layer_spec(P['wg'], l, 0), _layer_spec(P['br'], l, 0),
        _layer_spec(P['bi'], l, 0), _layer_spec(P['lam'], l, 0), state, _layer_spec(perm),
    ]
    f32_tok = jax.ShapeDtypeStruct((bsz, seq, D_LRU), F32)
    bf_tok = jax.ShapeDtypeStruct((bsz, seq, D_LRU), BF16)
    st = jax.ShapeDtypeStruct((bsz, 1, D_LRU), F32)
    if full:
        args += [pm, pinv, P['pw'], P['ps'], P['wpo']]
        in_specs += [_layer_spec(pm), _layer_spec(pinv), _layer_spec(P['pw'], l),
                     _layer_spec(P['ps'], l), _layer_spec(P['wpo'], l)]
        out_specs = [tok_spec, tok_spec, state, tok_spec, tok_spec, tok_spec]
        out_shape = [bf_tok, f32_tok, st, bf_tok, bf_tok, bf_tok]
    else:
        out_specs = [tok_spec, state]
        out_shape = [bf_tok, st]
    return pl.pallas_call(
        functools.partial(_fwd_kernel, tile=tile, n_tiles=n_tiles, full=full),
        grid=(bsz, n_tiles),
        in_specs=in_specs,
        out_specs=out_specs,
        out_shape=out_shape,
        scratch_shapes=[pltpu.VMEM((tile + 2 * HALO, d), BF16), pltpu.VMEM((1, D_LRU), F32)],
        compiler_params=pltpu.CompilerParams(
            dimension_semantics=("arbitrary", "arbitrary"), vmem_limit_bytes=VMEM_LIMIT),
        name=f"fwd_t{tile}" if full else f"fwdstate_t{tile}",
    )(*args)


def _bwd_kernel(*refs, tile, full):
    if full:
        (uc_ref, wg_ref, br_ref, bi_ref, lam_ref, h0_ref, x_ref, hf_ref, gg_ref, gl_ref, gp_ref,
         mod_ref, wlo_ref, wo_ref, unperm_ref, hfin_ref, out_ref, hl_s, carry_s) = refs
    else:
        (uc_ref, wg_ref, br_ref, bi_ref, lam_ref, h0_ref, hfin_ref, carry_s) = refs
    j = pl.program_id(1)
    nq = N_GATE_GROUPS
    qcols = _group_cols()

    @pl.when(j == 0)
    def _():
        carry_s[...] = h0_ref[0]

    nla_rate, l2a_rate = _decay_rates(lam_ref[0, 0])
    br, bi = br_ref[0, 0], bi_ref[0, 0]

    def coeffs(q):
        return _gate_coeffs(uc_ref[0, :, qcols[q]], wg_ref[0, 0, q], br[:, qcols[q]],
                            bi[:, qcols[q]], nla_rate[:, qcols[q]], l2a_rate[:, qcols[q]])

    def scan(q, ab):
        if full:
            def emit(r0, h):
                rws = slice(r0, r0 + h.shape[0])
                hl_s[rws, qcols[q]] = h + hf_ref[0, rws, qcols[q]]
        else:
            def emit(r0, h):
                del r0, h
        c = _scan_blocks(ab[0], ab[1], carry_s[:, qcols[q]], emit, reverse=True)
        carry_s[:, qcols[q]] = c
        hfin_ref[0, :, qcols[q]] = c

    ab = coeffs(0)
    lru_out = None
    for q in range(nq):
        nxt = coeffs(q + 1) if q + 1 < nq else None
        scan(q, ab)
        ab = nxt
        if full:
            lru_in = (hl_s[:, qcols[q]] * gg_ref[0, :, qcols[q]].astype(F32)).astype(BF16)
            part = jnp.dot(lru_in, wlo_ref[0, qcols[q], :], preferred_element_type=F32)
            lru_out = part if lru_out is None else lru_out + part
    if not full:
        return

    merged = gl_ref[0].astype(F32) * lru_out + gp_ref[0].astype(F32)
    yo = jnp.dot(_to_token_order(merged.astype(BF16), unperm_ref), wo_ref[0],
                 preferred_element_type=F32)
    out_ref[0] = x_ref[0] + mod_ref[0, 0][2:3] * yo


def _bwd_call(uc, h0, P, l, x=None, hf=None, gg=None, gl=None, gp=None, mods=None, mod_row=None,
              unperm=None, *, tile, full):
    bsz, seq, d = uc.shape
    n_tiles = seq // tile
    tok = lambda b, j: (b, n_tiles - 1 - j, 0)
    tok_spec = pl.BlockSpec((1, tile, d), tok)
    state = pl.BlockSpec((1, 1, D_LRU), lambda b, j: (b, 0, 0))
    args = [uc, P['wg'], P['br'], P['bi'], P['lam'], h0]
    in_specs = [tok_spec, _layer_spec(P['wg'], l, 1), _layer_spec(P['br'], l, 1),
                _layer_spec(P['bi'], l, 1), _layer_spec(P['lam'], l, 1), state]
    st = jax.ShapeDtypeStruct((bsz, 1, D_LRU), F32)
    scratch = [pltpu.VMEM((1, D_LRU), F32)]
    if full:
        args += [x, hf, gg, gl, gp, mods, P['wlo'], P['wo'], unperm]
        in_specs += [tok_spec] * 5 + [
            pl.BlockSpec((1, 1, N_MOD, d), lambda b, j: (l, mod_row(b), 0, 0)),
            _layer_spec(P['wlo'], l), _layer_spec(P['wo'], l), _layer_spec(unperm)]
        out_specs = [state, tok_spec]
        out_shape = [st, jax.ShapeDtypeStruct((bsz, seq, d), F32)]
        scratch = [pltpu.VMEM((tile, D_LRU), F32)] + scratch
    else:
        out_specs = state
        out_shape = st
    return pl.pallas_call(
        functools.partial(_bwd_kernel, tile=tile, full=full),
        grid=(bsz, n_tiles),
        in_specs=in_specs,
        out_specs=out_specs,
        out_shape=out_shape,
        scratch_shapes=scratch,
        compiler_params=pltpu.CompilerParams(
            dimension_semantics=("arbitrary", "arbitrary"), vmem_limit_bytes=VMEM_LIMIT),
        name=f"bwd_t{tile}" if full else f"bwdstate_t{tile}",
    )(*args)


def _mlp_kernel(x_ref, mod_ref, g2_ref, w1_ref, w2_ref, fg_ref, out_ref, hm_s, acc_s, *,
                n_chunks, final_norm):
    k = pl.program_id(2)

    @pl.when(k == 0)
    def _():
        mod = mod_ref[0, 0]
        hm_s[...] = _modulate(x_ref[0], g2_ref[0], mod[4:5], mod[3:4]).astype(BF16)
        acc_s[...] = jnp.zeros_like(acc_s)

    hid = jnp.dot(hm_s[...], w1_ref[0], preferred_element_type=F32)
    hid = jnp.square(jnp.maximum(hid, 0.0))
    acc_s[...] += jnp.dot(hid.astype(BF16), w2_ref[0], preferred_element_type=F32)

    @pl.when(k == n_chunks - 1)
    def _():
        y = x_ref[0] + mod_ref[0, 0][5:6] * acc_s[...]
        if final_norm:
            ms = jnp.mean(y * y, axis=-1, keepdims=True)
            y = y * lax.rsqrt(ms + EPS) * fg_ref[...]
        out_ref[0] = y


def _mlp_call(x, mods, mod_row, P, l, fg, *, tile, final_norm):
    bsz, seq, d = x.shape
    n_chunks = D_FF // FF_CHUNK
    tok = pl.BlockSpec((1, tile, d), lambda b, j, k: (b, j, 0))
    return pl.pallas_call(
        functools.partial(_mlp_kernel, n_chunks=n_chunks, final_norm=final_norm),
        grid=(bsz, seq // tile, n_chunks),
        in_specs=[
            tok,
            pl.BlockSpec((1, 1, N_MOD, d), lambda b, j, k: (l, mod_row(b), 0, 0)),
            _layer_spec(P['g2'], l),
            pl.BlockSpec((1, d, FF_CHUNK), lambda b, j, k: (l, 0, k)),
            pl.BlockSpec((1, FF_CHUNK, d), lambda b, j, k: (l, k, 0)),
            _layer_spec(fg),
        ],
        out_specs=tok,
        out_shape=jax.ShapeDtypeStruct((bsz, seq, d), F32),
        scratch_shapes=[pltpu.VMEM((tile, d), BF16), pltpu.VMEM((tile, d), F32)],
        compiler_params=pltpu.CompilerParams(
            dimension_semantics=("arbitrary", "arbitrary", "arbitrary"),
            vmem_limit_bytes=VMEM_LIMIT),
        name=f"mlp_t{tile}",
    )(x, mods, P['g2'], P['w1'], P['w2'], fg)


def _bwd_mlp_kernel(uc_ref, hf_ref, gg_ref, gl_ref, gp_ref, x_ref, wg_ref, br_ref, bi_ref, lam_ref,
                    h0_ref, mod_mix_ref, mod_mlp_ref, wlo_ref, wo_ref, unperm_ref, g2_ref, w1_ref,
                    w2_ref, fg_ref, out_ref, lru_s, x1_s, hm_s, hid_s, carry_s, *, tile, n_tiles,
                    final_norm):
    s = pl.program_id(0)
    n_steps = pl.num_programs(0) - 1

    @pl.when(s % n_tiles == 0)
    def _():
        carry_s[...] = h0_ref[0]

    args = (uc_ref, hf_ref, gg_ref, gl_ref, gp_ref, x_ref, wg_ref, br_ref, bi_ref, lam_ref,
            mod_mix_ref, mod_mlp_ref, wlo_ref, wo_ref, unperm_ref, g2_ref, w1_ref, w2_ref, fg_ref,
            out_ref, lru_s, x1_s, hm_s, hid_s, carry_s)
    step = functools.partial(_bwd_mlp_step, *args, tile=tile, final_norm=final_norm)
    pl.when(s == 0)(functools.partial(step, mixer=True, mlp=False))
    pl.when((s > 0) & (s < n_steps))(functools.partial(step, mixer=True, mlp=True))
    pl.when(s == n_steps)(functools.partial(step, mixer=False, mlp=True))


def _bwd_mlp_step(uc_ref, hf_ref, gg_ref, gl_ref, gp_ref, x_ref, wg_ref, br_ref, bi_ref, lam_ref,
                  mod_mix_ref, mod_mlp_ref, wlo_ref, wo_ref, unperm_ref, g2_ref, w1_ref, w2_ref,
                  fg_ref, out_ref, lru_s, x1_s, hm_s, hid_s, carry_s, *, tile, final_norm, mixer,
                  mlp):
    nq = N_GATE_GROUPS
    qcols = _group_cols()

    nla_rate, l2a_rate = _decay_rates(lam_ref[0, 0])
    br, bi = br_ref[0, 0], bi_ref[0, 0]
    state = {}

    half = ROW_BLOCK
    assert tile == 2 * half

    def coeffs(q, h):
        rws = slice(h * half, (h + 1) * half)
        state['ab', q, h] = _gate_coeffs(
            uc_ref[0, rws, qcols[q]], wg_ref[0, 0, q], br[:, qcols[q]], bi[:, qcols[q]],
            nla_rate[:, qcols[q]], l2a_rate[:, qcols[q]])

    def scan(q, h):
        def emit(r0, hrows):
            rws = slice(h * half + r0, h * half + r0 + hrows.shape[0])
            gated = (hrows + hf_ref[0, rws, qcols[q]]) * gg_ref[0, rws, qcols[q]].astype(F32)
            lru_s[rws, qcols[q]] = gated.astype(BF16)
        a, b = state.pop(('ab', q, h))
        carry_s[:, qcols[q]] = _scan_block(a, b, carry_s[:, qcols[q]], emit, reverse=True)

    def merge():
        lru_out = jnp.dot(lru_s[...], wlo_ref[0], preferred_element_type=F32)
        merged = gl_ref[0].astype(F32) * lru_out + gp_ref[0].astype(F32)
        state['merged'] = _to_token_order(merged.astype(BF16), unperm_ref)

    def project():
        yo = jnp.dot(state.pop('merged'), wo_ref[0], preferred_element_type=F32)
        state['x1'] = x_ref[0] + mod_mix_ref[0, 0][2:3] * yo

    mod = mod_mlp_ref[0, 0]

    def mlp_in():
        hm_s[...] = _modulate(x1_s[...], g2_ref[0], mod[4:5], mod[3:4]).astype(BF16)

    def mlp_up(k):
        cols = slice(k * MLP_FUSED_CHUNK, (k + 1) * MLP_FUSED_CHUNK)
        hid = jnp.dot(hm_s[...], w1_ref[0, :, cols], preferred_element_type=F32)
        hid_s[:, cols] = jnp.square(jnp.maximum(hid, 0.0)).astype(BF16)

    def mlp_down(n):
        cols = qcols[n]
        res = jnp.dot(hid_s[...], w2_ref[0, :, cols], preferred_element_type=F32)
        y = x1_s[:, cols] + mod[5:6, cols] * res
        out_ref[0, :, cols] = y
        if final_norm:
            part = jnp.sum(y * y, axis=-1, keepdims=True)
            state['ssq'] = part if n == 0 else state['ssq'] + part

    def mlp_out():
        if final_norm:
            scale = lax.rsqrt(state.pop('ssq') * (1.0 / D_MODEL) + EPS)
            out_ref[0] = out_ref[0] * scale * fg_ref[...]

    mxu_items = ([functools.partial(mlp_up, k) for k in range(D_FF // MLP_FUSED_CHUNK)]
                 + [functools.partial(mlp_down, n) for n in range(nq)])
    vec_items = [lambda: coeffs(0, 1), lambda: coeffs(0, 0)]
    for q in range(nq):
        vec_items.append(functools.partial(scan, q, 1))
        if q + 1 < nq:
            vec_items.append(functools.partial(coeffs, q + 1, 1))
        vec_items.append(functools.partial(scan, q, 0))
        if q + 1 < nq:
            vec_items.append(functools.partial(coeffs, q + 1, 0))
    ahead = [[], [], [], [], [], [], [merge], [project]]
    spread = (3, 3, 3, 3, 2, 2)
    assert sum(spread) == len(vec_items) and len(ahead) == len(mxu_items)
    it = iter(vec_items)
    for i, n_vec in enumerate(spread):
        ahead[i] = [next(it) for _ in range(n_vec)]
    if mixer and mlp:
        mlp_in()
        for vecs, mxu in zip(ahead, mxu_items):
            for vec in vecs:
                vec()
            mxu()
        mlp_out()
    elif mlp:
        mlp_in()
        for mxu in mxu_items:
            mxu()
        mlp_out()
    else:
        for vecs in ahead:
            for vec in vecs:
                vec()
    if mixer:
        x1_s[...] = state.pop('x1')


def _bwd_mlp_call(uc, hf, gg, gl, gp, x, h0, mods, P, l, unperm, fg, *, tile, final_norm):
    bsz, seq, d = x.shape
    n_tiles = seq // tile
    n_steps = bsz * n_tiles

    def tile_of(s):
        s = jnp.clip(s, 0, n_steps - 1)
        return s // n_tiles, n_tiles - 1 - s % n_tiles

    def tok(shift):
        def index_map(s):
            b, j = tile_of(s - shift)
            return (b, j, 0)
        return pl.BlockSpec((1, tile, d), index_map)

    def mod_spec(shift):
        return pl.BlockSpec((1, 1, N_MOD, d), lambda s: (l, tile_of(s - shift)[0], 0, 0))

    args = [uc, hf, gg, gl, gp, x, P['wg'], P['br'], P['bi'], P['lam'], h0, mods, mods,
            P['wlo'], P['wo'], unperm, P['g2'], P['w1'], P['w2'], fg]
    in_specs = [tok(0)] * 6 + [
        _layer_spec(P['wg'], l, 1), _layer_spec(P['br'], l, 1), _layer_spec(P['bi'], l, 1),
        _layer_spec(P['lam'], l, 1),
        pl.BlockSpec((1, 1, D_LRU), lambda s: (tile_of(s)[0], 0, 0)),
        mod_spec(0), mod_spec(1),
        _layer_spec(P['wlo'], l), _layer_spec(P['wo'], l), _layer_spec(unperm),
        _layer_spec(P['g2'], l), _layer_spec(P['w1'], l), _layer_spec(P['w2'], l), _layer_spec(fg)]
    return pl.pallas_call(
        functools.partial(_bwd_mlp_kernel, tile=tile, n_tiles=n_tiles, final_norm=final_norm),
        grid=(n_steps + 1,),
        in_specs=in_specs,
        out_specs=tok(1),
        out_shape=jax.ShapeDtypeStruct((bsz, seq, d), F32),
        scratch_shapes=[
            pltpu.VMEM((tile, D_LRU), BF16),
            pltpu.VMEM((tile, d), F32),
            pltpu.VMEM((tile, d), BF16),
            pltpu.VMEM((tile, D_FF), BF16),
            pltpu.VMEM((1, D_LRU), F32),
        ],
        compiler_params=pltpu.CompilerParams(
            dimension_semantics=("arbitrary",), vmem_limit_bytes=VMEM_LIMIT),
        name=f"bwdmlp_t{tile}",
    )(*args)


def _interleave_tables():
    r = np.arange(ROW_BLOCK)
    token = N_STEPS * (r % SUBLANES) + r // SUBLANES
    perm = np.zeros((ROW_BLOCK, ROW_BLOCK), np.float32)
    perm[r, token] = 1.0
    return token, jnp.asarray(perm, BF16), jnp.asarray(perm.T, BF16)


def _pool_tables(tile, row_len, token):
    assert ROW_BLOCK % row_len == 0 and tile % ROW_BLOCK == 0
    pos = token % row_len
    same_row = (token[:, None] // row_len) == (token[None, :] // row_len)
    mats, invs = [], []
    for w in POOL_WINDOWS:
        lo = np.clip(pos - w // 2, 0, row_len)
        hi = np.clip(pos + w - w // 2, 0, row_len)
        member = same_row & (pos[None, :] >= lo[:, None]) & (pos[None, :] < hi[:, None])
        mats.append(member.astype(np.float32))
        inv = np.repeat((1.0 / (hi - lo).astype(np.float32))[:, None], POOL_GROUP, axis=1)
        invs.append(np.tile(inv, (tile // ROW_BLOCK, 1)))
    return jnp.asarray(np.stack(mats), BF16), jnp.asarray(np.concatenate(invs, axis=1), F32)


def _gate_weights(w_r, w_i):
    hpg = GATE_GROUP // LRU_HEAD_DIM
    eye = jnp.eye(hpg, dtype=BF16)

    def blockdiag(w):
        w6 = w.astype(BF16).reshape(w.shape[0], 2, N_GATE_GROUPS, hpg, LRU_HEAD_DIM, LRU_HEAD_DIM)
        bd = jnp.einsum('ldqhie,hk->ldqhike', w6, eye)
        return bd.reshape(w.shape[0], 2, N_GATE_GROUPS, GATE_GROUP, GATE_GROUP)

    return jnp.concatenate([blockdiag(w_r), blockdiag(w_i)], axis=-1)


def kernel(x, c, ctx, c_ctx, w_ada, b_ada, norm1_g, norm2_g, w_in, conv_w, conv_b, lru_w_r, lru_b_r,
           lru_w_i, lru_b_i, lru_lambda, w_lru_out, pool_w, pool_scale, w_pool_out, w_o, mlp_w1,
           mlp_w2, final_g):
    bsz, seq, d = x.shape
    ctx_len = ctx.shape[1]
    depth = w_ada.shape[0]
    assert d == D_MODEL and seq % SEQ_TILE == 0
    assert bsz + 1 <= SUBLANES and ctx_len % HALO == 0

    cc = jnp.concatenate([c, c_ctx[None], jnp.zeros((SUBLANES - bsz - 1, d), F32)], axis=0)
    mods = _ada_call(cc, w_ada, b_ada).reshape(depth, SUBLANES, N_MOD, d)
    lat_row = lambda b: b
    ctx_row = lambda b: bsz

    token, perm, unperm = _interleave_tables()
    pm_lat, pinv_lat = _pool_tables(SEQ_TILE, GRID_W, token)
    pm_ctx, pinv_ctx = _pool_tables(ctx_len, ctx_len, token)
    zeros_state = jnp.zeros((bsz, 1, D_LRU), F32)
    fg = final_g.reshape(1, d)
    rows = lambda v: v.reshape(v.shape[:-1] + (1, v.shape[-1]))
    P = dict(
        g1=rows(norm1_g), g2=rows(norm2_g), w_in=w_in.astype(BF16), cw=conv_w, cb=rows(conv_b),
        wg=_gate_weights(lru_w_r, lru_w_i), br=rows(lru_b_r), bi=rows(lru_b_i),
        lam=rows(lru_lambda), pw=pool_w.astype(BF16), ps=rows(pool_scale),
        wpo=w_pool_out.astype(BF16), wlo=w_lru_out.astype(BF16), wo=w_o.astype(BF16),
        w1=mlp_w1.astype(BF16), w2=mlp_w2.astype(BF16))

    for l in range(depth):
        last = l == depth - 1
        if last:
            uc_c, hfin_f = _fwd_call(ctx, mods, ctx_row, zeros_state, P, l, perm, None, None,
                                     tile=ctx_len, full=False)
            hfin_b = _bwd_call(uc_c, zeros_state, P, l, tile=ctx_len, full=False)
        else:
            uc_c, hf_c, hfin_f, gg_c, gl_c, gp_c = _fwd_call(
                ctx, mods, ctx_row, zeros_state, P, l, perm, pm_ctx, pinv_ctx, tile=ctx_len,
                full=True)
            hfin_b, ctx = _bwd_call(uc_c, zeros_state, P, l, ctx, hf_c, gg_c, gl_c, gp_c, mods,
                                    ctx_row, unperm, tile=ctx_len, full=True)
            ctx = _mlp_call(ctx.reshape(1, bsz * ctx_len, d), mods, ctx_row, P, l, fg,
                            tile=bsz * ctx_len, final_norm=False).reshape(bsz, ctx_len, d)

        uc_l, hf_l, _, gg_l, gl_l, gp_l = _fwd_call(
            x, mods, lat_row, hfin_f, P, l, perm, pm_lat, pinv_lat, tile=SEQ_TILE, full=True)
        x = _bwd_mlp_call(uc_l, hf_l, gg_l, gl_l, gp_l, x, hfin_b, mods, P, l, unperm, fg,
                          tile=SEQ_TILE, final_norm=last)
    return x
```
